```python
import math
import jax, jax.numpy as jnp
from jax import lax
import numpy as np

D_MODEL = 4096
BATCH = 1
SEQ = 8192
DEPTH = 2

N_A_LAYERS = DEPTH // 2
N_B_LAYERS = DEPTH - N_A_LAYERS
N_DENSE_LAYERS = (DEPTH + 1) // 2
N_MOE_LAYERS = DEPTH // 2

RMS_EPS = 1e-6
L2_EPS = 1e-6
NEG_INF = -1e30

GDN_HEAD_DIM = 128
GDN_HEADS = D_MODEL // GDN_HEAD_DIM
GDN_WIDTH = GDN_HEADS * GDN_HEAD_DIM
GDN_CONV = 4
GDN_CHUNK = 64
GDN_PROJ = 4 * GDN_WIDTH + 2 * GDN_HEADS

MLA_NOPE_DIM = 128
MLA_ROPE_DIM = 64
MLA_V_DIM = 128
MLA_HEADS = D_MODEL // 128
MLA_Q_LORA = D_MODEL // 4
MLA_KV_LORA = 512
ROPE_THETA = 10000.0
ATTN_Q_BLOCK = 128

FFN_DIM = 11008
MOE_EXPERTS = 8
MOE_TOP_K = 2
MOE_FFN_DIM = 4096

ADA_CHUNKS = 6

kernel_name = 'yoco_gdn_mla_moe_adaln_trunk'


def rms_norm(x, gain):
    xf = x.astype(jnp.float32)
    y = xf * lax.rsqrt(jnp.mean(xf * xf, axis=-1, keepdims=True) + RMS_EPS)
    return (y * gain.astype(jnp.float32)).astype(x.dtype)


def l2_norm(x):
    xf = x.astype(jnp.float32)
    return (xf * lax.rsqrt(jnp.sum(xf * xf, axis=-1, keepdims=True) + L2_EPS)).astype(x.dtype)


def modulate(h, shift, scale):
    return h * (1.0 + scale) + shift


def swiglu(h, w_gate, w_up, w_down):
    return (jax.nn.silu(h @ w_gate) * (h @ w_up)) @ w_down


def rope_angles(positions, dim):
    inv_freq = ROPE_THETA ** (-jnp.arange(0, dim, 2, dtype=jnp.float32) / dim)
    ang = positions.astype(jnp.float32)[..., None] * inv_freq
    return jnp.cos(ang), jnp.sin(ang)


def apply_rope(x, cos, sin):
    half = x.shape[-1] // 2
    xf = x.astype(jnp.float32)
    x1, x2 = xf[..., :half], xf[..., half:]
    return jnp.concatenate([x1 * cos - x2 * sin, x1 * sin + x2 * cos], axis=-1).astype(x.dtype)


def causal_short_conv(x, w):
    K = w.shape[0]
    S = x.shape[1]
    xp = jnp.pad(x, ((0, 0), (K - 1, 0), (0, 0)))
    y = xp[:, 0:S] * w[0]
    for j in range(1, K):
        y = y + xp[:, j:j + S] * w[j]
    return jax.nn.silu(y)


def unit_lower_inverse(L):
    C = L.shape[-1]
    n = -L
    inv = jnp.eye(C, dtype=L.dtype) + n
    p = n
    for _ in range(int(math.log2(C)) - 1):
        p = p @ p
        inv = inv + inv @ p
    return inv


def chunk_gated_delta_rule(q, k, v, g, beta):
    B, S, H, dk = q.shape
    dv = v.shape[-1]
    C = GDN_CHUNK
    nc = S // C

    def to_chunks(a):
        a = a.astype(jnp.float32).reshape((B, nc, C, H) + a.shape[3:])
        return jnp.moveaxis(a, 3, 1)

    q = to_chunks(q) * (dk ** -0.5)
    k = to_chunks(k)
    v = to_chunks(v)
    beta = to_chunks(beta)
    g = jnp.cumsum(to_chunks(g), axis=-1)
    k_beta = k * beta[..., None]
    v_beta = v * beta[..., None]
    causal = jnp.tril(jnp.ones((C, C), dtype=bool))
    strict = jnp.tril(jnp.ones((C, C), dtype=bool), -1)
    gdiff = g[..., :, None] - g[..., None, :]
    decay = jnp.where(causal, jnp.exp(jnp.where(causal, gdiff, 0.0)), 0.0)
    L = jnp.where(strict, jnp.einsum('bhncd,bhnsd->bhncs', k_beta, k) * decay, 0.0)
    T = unit_lower_inverse(L)
    u = T @ v_beta
    w = T @ (k_beta * jnp.exp(g)[..., None])
    qk = jnp.einsum('bhncd,bhnsd->bhncs', q, k) * decay
    q_dec = q * jnp.exp(g)[..., None]
    k_to_end = k * jnp.exp(g[..., -1:] - g)[..., None]
    g_end = jnp.exp(g[..., -1])

    def step(state, inp):
        qd_i, kt_i, u_i, w_i, qk_i, ge_i = inp
        v_new = u_i - w_i @ state
        o_i = qd_i @ state + qk_i @ v_new
        state = state * ge_i[..., None, None] + jnp.einsum('bhcd,bhce->bhde', kt_i, v_new)
        return state, o_i

    xs = (jnp.moveaxis(q_dec, 2, 0), jnp.moveaxis(k_to_end, 2, 0), jnp.moveaxis(u, 2, 0),
          jnp.moveaxis(w, 2, 0), jnp.moveaxis(qk, 2, 0), jnp.moveaxis(g_end, 2, 0))
    state0 = jnp.zeros((B, H, dk, dv), jnp.float32)
    _, o = lax.scan(step, state0, xs)
    o = jnp.moveaxis(o, 0, 2).reshape(B, H, S, dv)
    return jnp.moveaxis(o, 1, 2)


def gated_deltanet(h, w_in, conv_w, a_log, dt_bias, out_norm, w_out):
    B, S, _ = h.shape
    proj = h @ w_in
    qkv, z, b_raw, a_raw = jnp.split(
        proj, [3 * GDN_WIDTH, 4 * GDN_WIDTH, 4 * GDN_WIDTH + GDN_HEADS], axis=-1)
    qkv = causal_short_conv(qkv, conv_w)
    q, k, v = jnp.split(qkv, 3, axis=-1)
    q = l2_norm(q.reshape(B, S, GDN_HEADS, GDN_HEAD_DIM))
    k = l2_norm(k.reshape(B, S, GDN_HEADS, GDN_HEAD_DIM))
    v = v.reshape(B, S, GDN_HEADS, GDN_HEAD_DIM)
    beta = jax.nn.sigmoid(b_raw.astype(jnp.float32))
    g = -jnp.exp(a_log.astype(jnp.float32)) * jax.nn.softplus(
        a_raw.astype(jnp.float32) + dt_bias.astype(jnp.float32))
    o = chunk_gated_delta_rule(q, k, v, g, beta).astype(h.dtype)
    o = rms_norm(o, out_norm) * jax.nn.silu(z.reshape(B, S, GDN_HEADS, GDN_HEAD_DIM))
    return o.reshape(B, S, GDN_WIDTH) @ w_out


def shared_latent_kv(x, kv_norm, w_dkv, kv_latent_norm, w_ukv, k_nope_norm, k_rope_norm, cos, sin):
    B, S, _ = x.shape
    h = rms_norm(x, kv_norm)
    ckv, k_rope = jnp.split(h @ w_dkv, [MLA_KV_LORA], axis=-1)
    ckv = rms_norm(ckv, kv_latent_norm)
    kv = (ckv @ w_ukv).reshape(B, S, MLA_HEADS, MLA_NOPE_DIM + MLA_V_DIM)
    k_nope, v = jnp.split(kv, [MLA_NOPE_DIM], axis=-1)
    k_nope = rms_norm(k_nope, k_nope_norm)
    k_rope = apply_rope(rms_norm(k_rope, k_rope_norm), cos, sin)
    return k_nope, k_rope, v


def causal_latent_attention(q_nope, q_rope, k_nope, k_rope, v):
    B, S, H, _ = q_nope.shape
    nb = S // ATTN_Q_BLOCK
    scale = (MLA_NOPE_DIM + MLA_ROPE_DIM) ** -0.5

    def blocks(a):
        return jnp.moveaxis(a.reshape((B, nb, ATTN_Q_BLOCK) + a.shape[2:]), 1, 0)

    key_pos = jnp.arange(S)

    def attend(args):
        qn, qr, start = args
        s = (jnp.einsum('bqhd,bkhd->bhqk', qn, k_nope)
             + jnp.einsum('bqhd,bkd->bhqk', qr, k_rope)).astype(jnp.float32) * scale
        q_pos = start + jnp.arange(ATTN_Q_BLOCK)
        s = jnp.where(key_pos[None, :] <= q_pos[:, None], s, NEG_INF)
        p = jax.nn.softmax(s, axis=-1).astype(v.dtype)
        return jnp.einsum('bhqk,bkhd->bqhd', p, v)

    o = lax.map(attend, (blocks(q_nope), blocks(q_rope), jnp.arange(nb) * ATTN_Q_BLOCK))
    return jnp.moveaxis(o, 0, 1).reshape(B, S, H, MLA_V_DIM)


def mla_mixer(h, k_nope, k_rope, v, w_dq, q_latent_norm, w_uq, q_nope_norm, q_rope_norm, w_out, cos, sin):
    B, S, _ = h.shape
    cq = rms_norm(h @ w_dq, q_latent_norm)
    q = (cq @ w_uq).reshape(B, S, MLA_HEADS, MLA_NOPE_DIM + MLA_ROPE_DIM)
    q_nope, q_rope = jnp.split(q, [MLA_NOPE_DIM], axis=-1)
    q_nope = rms_norm(q_nope, q_nope_norm)
    q_rope = apply_rope(rms_norm(q_rope, q_rope_norm), cos[:, :, None, :], sin[:, :, None, :])
    o = causal_latent_attention(q_nope, q_rope, k_nope, k_rope, v)
    return o.reshape(B, S, MLA_HEADS * MLA_V_DIM) @ w_out


def moe_swiglu(h, router_w, router_b, w_gate, w_up, w_down):
    logits = (h @ router_w).astype(jnp.float32) + router_b.astype(jnp.float32)
    probs = jax.nn.softmax(logits, axis=-1)
    top_p, top_i = lax.top_k(probs, MOE_TOP_K)
    top_p = top_p / jnp.sum(top_p, axis=-1, keepdims=True)
    combine = jnp.einsum('bsk,bske->bse', top_p,
                         jax.nn.one_hot(top_i, MOE_EXPERTS, dtype=jnp.float32)).astype(h.dtype)
    out = jnp.zeros_like(h)
    for e in range(MOE_EXPERTS):
        out = out + combine[..., e:e + 1] * swiglu(h, w_gate[e], w_up[e], w_down[e])
    return out


def setup_inputs(seed: int = 0) -> dict:
    key = jax.random.key(seed)
    ks = iter(jax.random.split(key, 48))
    f32 = jnp.float32

    def w(shape, fan_in, scale=1.0):
        return jax.random.normal(next(ks), shape, f32) * (scale * fan_in ** -0.5)

    def gain(shape):
        return 1.0 + 0.1 * jax.random.normal(next(ks), shape, f32)

    def bias(shape, s=0.01):
        return s * jax.random.normal(next(ks), shape, f32)

    x = jax.random.normal(next(ks), (BATCH, SEQ, D_MODEL), f32)
    c = jax.random.normal(next(ks), (BATCH, D_MODEL), f32)
    offset = jax.random.randint(next(ks), (BATCH, 1), 0, 4096, dtype=jnp.int32)
    positions = (jnp.arange(SEQ, dtype=jnp.int32)[None, :] + offset).astype(jnp.int32)

    a_log = jnp.log(jax.random.uniform(next(ks), (N_A_LAYERS, GDN_HEADS), f32, 1.0, 16.0))
    dt = jnp.exp(jax.random.uniform(next(ks), (N_A_LAYERS, GDN_HEADS), f32,
                                    math.log(1e-3), math.log(1e-1)))
    dt_bias = dt + jnp.log(-jnp.expm1(-dt))

    return {
        'x': x,
        'c': c,
        'positions': positions,
        'ada_w': w((DEPTH, D_MODEL, ADA_CHUNKS * D_MODEL), D_MODEL, 0.5),
        'ada_b': bias((DEPTH, ADA_CHUNKS * D_MODEL), 0.02),
        'norm_mix': gain((DEPTH, D_MODEL)),
        'norm_ffn': gain((DEPTH, D_MODEL)),
        'gdn_w_in': w((N_A_LAYERS, D_MODEL, GDN_PROJ), D_MODEL),
        'gdn_conv': w((N_A_LAYERS, GDN_CONV, 3 * GDN_WIDTH), GDN_CONV),
        'gdn_a_log': a_log,
        'gdn_dt_bias': dt_bias,
        'gdn_out_norm': gain((N_A_LAYERS, GDN_HEAD_DIM)),
        'gdn_w_out': w((N_A_LAYERS, GDN_WIDTH, D_MODEL), GDN_WIDTH),
        'kv_norm': gain((D_MODEL,)),
        'w_dkv': w((D_MODEL, MLA_KV_LORA + MLA_ROPE_DIM), D_MODEL),
        'kv_latent_norm': gain((MLA_KV_LORA,)),
        'w_ukv': w((MLA_KV_LORA, MLA_HEADS * (MLA_NOPE_DIM + MLA_V_DIM)), MLA_KV_LORA),
        'k_nope_norm': gain((MLA_NOPE_DIM,)),
        'k_rope_norm': gain((MLA_ROPE_DIM,)),
        'mla_w_dq': w((N_B_LAYERS, D_MODEL, MLA_Q_LORA), D_MODEL),
        'mla_q_latent_norm': gain((N_B_LAYERS, MLA_Q_LORA)),
        'mla_w_uq': w((N_B_LAYERS, MLA_Q_LORA, MLA_HEADS * (MLA_NOPE_DIM + MLA_ROPE_DIM)), MLA_Q_LORA),
        'mla_q_nope_norm': gain((N_B_LAYERS, MLA_NOPE_DIM)),
        'mla_q_rope_norm': gain((N_B_LAYERS, MLA_ROPE_DIM)),
        'mla_w_out': w((N_B_LAYERS, MLA_HEADS * MLA_V_DIM, D_MODEL), MLA_HEADS * MLA_V_DIM),
        'ffn_w_gate': w((N_DENSE_LAYERS, D_MODEL, FFN_DIM), D_MODEL),
        'ffn_w_up': w((N_DENSE_LAYERS, D_MODEL, FFN_DIM), D_MODEL),
        'ffn_w_down': w((N_DENSE_LAYERS, FFN_DIM, D_MODEL), FFN_DIM),
        'router_w': w((N_MOE_LAYERS, D_MODEL, MOE_EXPERTS), D_MODEL),
        'router_b': bias((N_MOE_LAYERS, MOE_EXPERTS), 0.01),
        'moe_w_gate': w((N_MOE_LAYERS, MOE_EXPERTS, D_MODEL, MOE_FFN_DIM), D_MODEL),
        'moe_w_up': w((N_MOE_LAYERS, MOE_EXPERTS, D_MODEL, MOE_FFN_DIM), D_MODEL),
        'moe_w_down': w((N_MOE_LAYERS, MOE_EXPERTS, MOE_FFN_DIM, D_MODEL), MOE_FFN_DIM),
    }


def reference(x, c, positions, ada_w, ada_b, norm_mix, norm_ffn,
              gdn_w_in, gdn_conv, gdn_a_log, gdn_dt_bias, gdn_out_norm, gdn_w_out,
              kv_norm, w_dkv, kv_latent_norm, w_ukv, k_nope_norm, k_rope_norm,
              mla_w_dq, mla_q_latent_norm, mla_w_uq, mla_q_nope_norm, mla_q_rope_norm, mla_w_out,
              ffn_w_gate, ffn_w_up, ffn_w_down,
              router_w, router_b, moe_w_gate, moe_w_up, moe_w_down):
    cos, sin = rope_angles(positions, MLA_ROPE_DIM)
    c_act = jax.nn.silu(c)
    shared_kv = None
    for layer in range(DEPTH):
        mod = (c_act @ ada_w[layer] + ada_b[layer])[:, None, :]
        sh_m, sc_m, g_m, sh_f, sc_f, g_f = jnp.split(mod, ADA_CHUNKS, axis=-1)
        h = modulate(rms_norm(x, norm_mix[layer]), sh_m, sc_m)
        if layer < N_A_LAYERS:
            a = layer
            mix = gated_deltanet(h, gdn_w_in[a], gdn_conv[a], gdn_a_log[a], gdn_dt_bias[a],
                                 gdn_out_norm[a], gdn_w_out[a])
        else:
            if layer == N_A_LAYERS:
                shared_kv = shared_latent_kv(x, kv_norm, w_dkv, kv_latent_norm, w_ukv,
                                             k_nope_norm, k_rope_norm, cos, sin)
            b = layer - N_A_LAYERS
            k_nope, k_rope, v = shared_kv
            mix = mla_mixer(h, k_nope, k_rope, v, mla_w_dq[b], mla_q_latent_norm[b], mla_w_uq[b],
                            mla_q_nope_norm[b], mla_q_rope_norm[b], mla_w_out[b], cos, sin)
        x = x + g_m * mix
        h = modulate(rms_norm(x, norm_ffn[layer]), sh_f, sc_f)
        if layer % 2 == 0:
            d = layer // 2
            ffn = swiglu(h, ffn_w_gate[d], ffn_w_up[d], ffn_w_down[d])
        else:
            m = layer // 2
            ffn = moe_swiglu(h, router_w[m], router_b[m], moe_w_gate[m], moe_w_up[m], moe_w_down[m])
        x = x + g_f * ffn
    return x
```

```python
import functools
import math

import jax
import jax.numpy as jnp
from jax import lax
from jax.experimental import pallas as pl
from jax.experimental.pallas import tpu as pltpu

F32 = jnp.float32
BF16 = jnp.bfloat16
HIGHEST = lax.Precision.HIGHEST

RMS_EPS = 1e-6
L2_EPS = 1e-6
NEG_INF = -1e30
ROPE_THETA = 10000.0

LANES = 128
HEAD_DIM = 128
ROPE_DIM = 64
GDN_CHUNK = 64
GDN_CONV = 4
MOE_TOP_K = 2
ADA_CHUNKS = 6
HALO_ROWS = 16
VMEM_LIMIT_BYTES = 56 * 1024 * 1024


def _params(*semantics):
    return pltpu.CompilerParams(dimension_semantics=semantics, vmem_limit_bytes=VMEM_LIMIT_BYTES)


def _dot(a, b):
    return jnp.dot(a, b, preferred_element_type=F32)


def _dot_nt(a, b):
    return lax.dot_general(a, b, (((1,), (1,)), ((), ())), preferred_element_type=F32)


def _dot_tn(a, b):
    return lax.dot_general(a, b, (((0,), (0,)), ((), ())), preferred_element_type=F32)


def _silu(x):
    return x * jax.nn.sigmoid(x)


def _rms(x, gain, n):
    ms = jnp.sum(x * x, axis=-1, keepdims=True) * (1.0 / n)
    return x * lax.rsqrt(ms + RMS_EPS) * gain


def _ada_body(c_ref, w_ref, b_ref, o_ref):
    cs = _silu(c_ref[...])
    for j in range(o_ref.shape[-1] // LANES):
        sl = slice(j * LANES, (j + 1) * LANES)
        s = jnp.sum(w_ref[0, :, sl] * cs, axis=0, keepdims=True)
        o_ref[0, :, sl] = s + b_ref[0, :, sl]


def _ada_mod(c, ada_w, ada_b, tn=1024):
    depth, d, n = ada_w.shape
    c_b = jnp.broadcast_to(c.reshape(d, 1), (d, LANES))
    return pl.pallas_call(
        _ada_body,
        grid=(depth, n // tn),
        in_specs=[pl.BlockSpec((d, LANES), lambda l, j: (0, 0)),
                  pl.BlockSpec((1, d, tn), lambda l, j: (l, 0, j)),
                  pl.BlockSpec((1, 1, tn), lambda l, j: (l, 0, j))],
        out_specs=pl.BlockSpec((1, 1, tn), lambda l, j: (l, 0, j)),
        out_shape=jax.ShapeDtypeStruct((depth, 1, n), F32),
        compiler_params=_params("parallel", "parallel"),
    )(c_b, ada_w, ada_b.reshape(depth, 1, n))


def _norm_body(x_ref, g_ref, sc_ref, sh_ref, *o_refs, modulate, packed):
    x = x_ref[...]
    d = x.shape[-1]
    y = _rms(x, g_ref[...], d)
    if modulate:
        y = y * (1.0 + sc_ref[...]) + sh_ref[...]
    yb = y.astype(BF16)
    o_refs[0][...] = yb
    if packed:
        bits = lax.bitcast_convert_type(yb.astype(F32), jnp.uint32)
        lo = bits[:, :d // 2] >> 16
        hi = bits[:, d // 2:] & jnp.uint32(0xFFFF0000)
        o_refs[1][...] = hi | lo


def _norm_mod(x, gain, scale, shift, *, name, modulate=True, packed=False, tr=256):
    s, d = x.shape
    tr = min(tr, s)
    vec = pl.BlockSpec((1, d), lambda i: (0, 0))
    out_shape = [jax.ShapeDtypeStruct((s, d), BF16)]
    out_specs = [pl.BlockSpec((tr, d), lambda i: (i, 0))]
    if packed:
        out_shape.append(jax.ShapeDtypeStruct((s, d // 2), jnp.uint32))
        out_specs.append(pl.BlockSpec((tr, d // 2), lambda i: (i, 0)))
    outs = pl.pallas_call(
        functools.partial(_norm_body, modulate=modulate, packed=packed),
        grid=(s // tr,),
        in_specs=[pl.BlockSpec((tr, d), lambda i: (i, 0)), vec, vec, vec],
        out_specs=out_specs,
        out_shape=out_shape,
        compiler_params=_params("parallel"),
        name=name,
    )(x, gain.reshape(1, d), scale.reshape(1, d), shift.reshape(1, d))
    return outs if packed else outs[0]


def _mm_body(a_ref, w_ref, o_ref):
    o_ref[...] = _dot(a_ref[...], w_ref[...].astype(BF16)).astype(o_ref.dtype)


def _mm_res_body(a_ref, w_ref, r_ref, g_ref, o_ref):
    o_ref[...] = r_ref[...] + g_ref[...] * _dot(a_ref[...], w_ref[...].astype(BF16))


def _matmul(a, w, *, name, out_dtype, tm=1024, tn=512, res=None, gate=None):
    m, k = a.shape
    n = w.shape[1]
    tm, tn = min(tm, m), min(tn, n)
    in_specs = [pl.BlockSpec((tm, k), lambda i, j: (i, 0)),
                pl.BlockSpec((k, tn), lambda i, j: (0, j))]
    args = [a, w]
    body = _mm_body
    if res is not None:
        in_specs += [pl.BlockSpec((tm, tn), lambda i, j: (i, j)),
                     pl.BlockSpec((1, tn), lambda i, j: (0, j))]
        args += [res, gate.reshape(1, n)]
        body = _mm_res_body
    return pl.pallas_call(
        body,
        grid=(m // tm, n // tn),
        in_specs=in_specs,
        out_specs=pl.BlockSpec((tm, tn), lambda i, j: (i, j)),
        out_shape=jax.ShapeDtypeStruct((m, n), out_dtype),
        compiler_params=_params("parallel", "parallel"),
        name=name,
    )(*args)


def _mm_ktiled_res_body(a_ref, w_ref, r_ref, g_ref, o_ref, acc_ref):
    kk = pl.program_id(2)

    @pl.when(kk == 0)
    def _():
        acc_ref[...] = jnp.zeros_like(acc_ref)

    acc_ref[...] += _dot(a_ref[...], w_ref[...].astype(BF16))

    @pl.when(kk == pl.num_programs(2) - 1)
    def _():
        o_ref[...] = r_ref[...] + g_ref[...] * acc_ref[...]


def _matmul_ktiled_res(a, w, res, gate, *, tm=512, tn=1024, tk):
    m, k = a.shape
    n = w.shape[1]
    tm, tn = min(tm, m), min(tn, n)
    return pl.pallas_call(
        _mm_ktiled_res_body,
        grid=(m // tm, n // tn, k // tk),
        in_specs=[pl.BlockSpec((tm, tk), lambda i, j, kk: (i, kk)),
                  pl.BlockSpec((tk, tn), lambda i, j, kk: (kk, j)),
                  pl.BlockSpec((tm, tn), lambda i, j, kk: (i, j)),
                  pl.BlockSpec((1, tn), lambda i, j, kk: (0, j))],
        out_specs=pl.BlockSpec((tm, tn), lambda i, j, kk: (i, j)),
        out_shape=jax.ShapeDtypeStruct((m, n), F32),
        scratch_shapes=[pltpu.VMEM((tm, tn), F32)],
        compiler_params=_params("parallel", "parallel", "arbitrary"),
    )(a, w, res, gate.reshape(1, n))


def _swiglu_up_body(a_ref, wg_ref, wu_ref, o_ref):
    a = a_ref[...]
    g = _dot(a, wg_ref[...].astype(BF16))
    u = _dot(a, wu_ref[...].astype(BF16))
    o_ref[...] = (_silu(g) * u).astype(o_ref.dtype)


def _swiglu_up(a, wg, wu, *, tm=1024, tn=256):
    m, k = a.shape
    n = wg.shape[1]
    tm = min(tm, m)
    wspec = pl.BlockSpec((k, tn), lambda i, j: (0, j))
    return pl.pallas_call(
        _swiglu_up_body,
        grid=(m // tm, n // tn),
        in_specs=[pl.BlockSpec((tm, k), lambda i, j: (i, 0)), wspec, wspec],
        out_specs=pl.BlockSpec((tm, tn), lambda i, j: (i, j)),
        out_shape=jax.ShapeDtypeStruct((m, n), BF16),
        compiler_params=_params("parallel", "parallel"),
    )(a, wg, wu)


def _gdn_conv_body(halo_ref, x_ref, w_ref, o_ref, buf_ref):
    i = pl.program_id(0)
    which = pl.program_id(1)
    tr = x_ref.shape[0]
    halo = halo_ref[...].astype(F32)
    buf_ref[0:HALO_ROWS, :] = jnp.where(i > 0, halo, 0.0)
    buf_ref[HALO_ROWS:HALO_ROWS + tr, :] = x_ref[...].astype(F32)
    for h in range(x_ref.shape[1] // HEAD_DIM):
        sl = slice(h * HEAD_DIM, (h + 1) * HEAD_DIM)
        y = None
        for j in range(GDN_CONV):
            start = HALO_ROWS - (GDN_CONV - 1) + j
            term = buf_ref[start:start + tr, sl] * w_ref[j:j + 1, sl]
            y = term if y is None else y + term
        y = _silu(y)
        ss = jnp.sum(y * y, axis=-1, keepdims=True)
        yn = y * lax.rsqrt(ss + L2_EPS)
        o_ref[:, sl] = jnp.where(which < 2, yn, y).astype(o_ref.dtype)


def _gdn_conv(proj, conv_w, width, tr=256):
    s = proj.shape[0]
    tr = min(tr, s)
    hb = tr // HALO_ROWS
    return pl.pallas_call(
        _gdn_conv_body,
        grid=(s // tr, 3),
        in_specs=[pl.BlockSpec((HALO_ROWS, width), lambda i, c: (jnp.maximum(i * hb - 1, 0), c)),
                  pl.BlockSpec((tr, width), lambda i, c: (i, c)),
                  pl.BlockSpec((GDN_CONV, width), lambda i, c: (0, c))],
        out_specs=pl.BlockSpec((tr, width), lambda i, c: (i, c)),
        out_shape=jax.ShapeDtypeStruct((s, 3 * width), BF16),
        scratch_shapes=[pltpu.VMEM((HALO_ROWS + tr, width), F32)],
        compiler_params=_params("parallel", "parallel"),
    )(proj, proj, conv_w)


def _gdn_gate_body(b_ref, a_ref, alog_ref, dtb_ref, gcb_ref, bb_ref, gct_ref):
    tr = b_ref.shape[0]
    beta = jax.nn.sigmoid(b_ref[...])
    x = a_ref[...] + dtb_ref[...]
    softplus = jnp.maximum(x, 0.0) + jnp.log1p(jnp.exp(-jnp.abs(x)))
    g = -jnp.exp(alog_ref[...]) * softplus
    r = lax.broadcasted_iota(jnp.int32, (tr, tr), 0)
    c = lax.broadcasted_iota(jnp.int32, (tr, tr), 1)
    shift = int(math.log2(GDN_CHUNK))
    tri = ((r >> shift == c >> shift) & (c <= r)).astype(F32)
    gc = jnp.dot(tri, g, precision=HIGHEST, preferred_element_type=F32)
    width = gcb_ref.shape[1]
    er = lax.broadcasted_iota(jnp.int32, (LANES, width), 0)
    ec = lax.broadcasted_iota(jnp.int32, (LANES, width), 1)
    expand = (ec >> int(math.log2(HEAD_DIM)) == er).astype(F32)
    gcb_ref[...] = jnp.dot(gc, expand, precision=HIGHEST, preferred_element_type=F32)
    bb_ref[...] = jnp.dot(beta, expand, precision=HIGHEST, preferred_element_type=F32)
    ir = lax.broadcasted_iota(jnp.int32, (LANES, LANES), 0)
    ic = lax.broadcasted_iota(jnp.int32, (LANES, LANES), 1)
    eye = (ir == ic).astype(F32)
    gct_ref[...] = lax.dot_general(eye, gc, (((1,), (1,)), ((), ())), precision=HIGHEST,
                                   preferred_element_type=F32)


def _gdn_gate(b_raw, a_raw, a_log, dt_bias, width, tr=512):
    s = b_raw.shape[0]
    tr = min(tr, s)
    row = pl.BlockSpec((tr, LANES), lambda i: (i, 0))
    vec = pl.BlockSpec((1, LANES), lambda i: (0, 0))
    wide = pl.BlockSpec((tr, width), lambda i: (i, 0))
    return pl.pallas_call(
        _gdn_gate_body,
        grid=(s // tr,),
        in_specs=[row, row, vec, vec],
        out_specs=[wide, wide, pl.BlockSpec((LANES, tr), lambda i: (0, i))],
        out_shape=[jax.ShapeDtypeStruct((s, width), F32), jax.ShapeDtypeStruct((s, width), F32),
                   jax.ShapeDtypeStruct((LANES, s), F32)],
        compiler_params=_params("parallel"),
    )(b_raw, a_raw, a_log, dt_bias)


def _unit_lower_inverse(low, eye):
    n = -low
    inv = eye + n
    p = n
    for _ in range(int(math.log2(low.shape[-1])) - 1):
        p16 = p.astype(BF16)
        p = _dot(p16, p16)
        inv = inv + _dot(inv.astype(BF16), p.astype(BF16))
    return inv


def _gdn_chunk_body(q_ref, k_ref, v_ref, gcb_ref, bb_ref, gct_ref, z_ref, gain_ref, o_ref, state_ref):
    @pl.when(pl.program_id(1) == 0)
    def _():
        state_ref[...] = jnp.zeros_like(state_ref)

    rows = q_ref.shape[0]
    cs = GDN_CHUNK
    r = lax.broadcasted_iota(jnp.int32, (cs, cs), 0)
    c = lax.broadcasted_iota(jnp.int32, (cs, cs), 1)
    causal = c <= r
    strict = c < r
    eye = (r == c).astype(F32)
    pre = []
    for n in range(rows // cs):
        sl = slice(n * cs, (n + 1) * cs)
        q = q_ref[sl, :].astype(F32) * (HEAD_DIM ** -0.5)
        k = k_ref[sl, :].astype(F32)
        v = v_ref[sl, :].astype(F32)
        gc = gcb_ref[sl, :]
        beta = bb_ref[sl, :]
        gc_row = gct_ref[:, sl]
        eg = jnp.exp(gc)
        kb = k * beta
        gdiff = gc[:, :cs] - gc_row
        decay = jnp.where(causal, jnp.exp(jnp.where(causal, gdiff, 0.0)), 0.0)
        k16 = k.astype(BF16)
        low = jnp.where(strict, _dot_nt(kb.astype(BF16), k16) * decay, 0.0)
        t16 = _unit_lower_inverse(low, eye).astype(BF16)
        u = _dot(t16, (v * beta).astype(BF16))
        w = _dot(t16, (kb * eg).astype(BF16))
        qk = _dot_nt(q.astype(BF16), k16) * decay
        g_last = gc[cs - 1:cs, :]
        pre.append((u, w, qk, q * eg, k * jnp.exp(g_last - gc), jnp.exp(g_last)))
    state = state_ref[...]
    for n in range(rows // cs):
        sl = slice(n * cs, (n + 1) * cs)
        u, w, qk, q_dec, k_end, g_end = pre[n]
        s16 = state.astype(BF16)
        v_new = u - _dot(w.astype(BF16), s16)
        v16 = v_new.astype(BF16)
        o = _dot(q_dec.astype(BF16), s16) + _dot(qk.astype(BF16), v16)
        state = state * g_end + _dot_tn(k_end.astype(BF16), v16)
        z = z_ref[sl, :].astype(F32)
        o_ref[sl, :] = (_rms(o, gain_ref[...], HEAD_DIM) * _silu(z)).astype(o_ref.dtype)
    state_ref[...] = state


def _gdn_chunk(qkv, proj, gcb, bb, gct, out_norm, heads, rows=512):
    s = qkv.shape[0]
    rows = min(rows, s)
    blk = lambda off: pl.BlockSpec((rows, HEAD_DIM), lambda h, r: (r, off + h))
    return pl.pallas_call(
        _gdn_chunk_body,
        grid=(heads, s // rows),
        in_specs=[blk(0), blk(heads), blk(2 * heads),
                  pl.BlockSpec((rows, HEAD_DIM), lambda h, r: (r, h)),
                  pl.BlockSpec((rows, HEAD_DIM), lambda h, r: (r, h)),
                  pl.BlockSpec((None, 1, rows), lambda h, r: (h, 0, r)),
                  blk(3 * heads),
                  pl.BlockSpec((1, HEAD_DIM), lambda h, r: (0, 0))],
        out_specs=pl.BlockSpec((rows, HEAD_DIM), lambda h, r: (r, h)),
        out_shape=jax.ShapeDtypeStruct((s, heads * HEAD_DIM), BF16),
        scratch_shapes=[pltpu.VMEM((HEAD_DIM, HEAD_DIM), F32)],
        compiler_params=_params("parallel", "arbitrary"),
    )(qkv, qkv, qkv, gcb, bb, gct.reshape(gct.shape[0], 1, s), proj, out_norm.reshape(1, HEAD_DIM))


def _rope(x, cos_ref, sin_a_ref, sin_b_ref):
    half = ROPE_DIM // 2
    return (x * cos_ref[...] + pltpu.roll(x, LANES - half, 1) * sin_a_ref[...]
            + pltpu.roll(x, half, 1) * sin_b_ref[...])


def _kv_down_body(a_ref, w1_ref, w2_ref, g1_ref, g2_ref, cos_ref, sa_ref, sb_ref, ckv_ref, kr_ref):
    a = a_ref[...]
    ckv = _dot(a, w1_ref[...].astype(BF16))
    ckv_ref[...] = _rms(ckv, g1_ref[...], ckv.shape[-1]).astype(ckv_ref.dtype)
    kr = _rms(_dot(a, w2_ref[...].astype(BF16)), g2_ref[...], ROPE_DIM)
    kr_ref[...] = _rope(kr, cos_ref, sa_ref, sb_ref).astype(kr_ref.dtype)


def _kv_down(a, w_c, w_r, g_c, g_r, rope_tabs, tm=512):
    m, k = a.shape
    lat = w_c.shape[1]
    tm = min(tm, m)
    full = lambda shp: pl.BlockSpec(shp, lambda i: (0, 0))
    row = lambda w: pl.BlockSpec((tm, w), lambda i: (i, 0))
    return pl.pallas_call(
        _kv_down_body,
        grid=(m // tm,),
        in_specs=[row(k), full((k, lat)), full((k, LANES)), full((1, lat)), full((1, LANES)),
                  row(LANES), row(LANES), row(LANES)],
        out_specs=[row(lat), row(LANES)],
        out_shape=[jax.ShapeDtypeStruct((m, lat), BF16), jax.ShapeDtypeStruct((m, LANES), BF16)],
        compiler_params=_params("parallel"),
    )(a, w_c, w_r, g_c, g_r, *rope_tabs)


def _kv_up_body(a_ref, w_ref, kr_ref, g_ref, k_ref, v_ref):
    acc = _dot(a_ref[...], w_ref[...].astype(BF16))
    for h in range(acc.shape[1] // (2 * HEAD_DIM)):
        base = 2 * HEAD_DIM * h
        kn = acc[:, base:base + HEAD_DIM]
        k_ref[:, base:base + HEAD_DIM] = _rms(kn, g_ref[...], HEAD_DIM).astype(k_ref.dtype)
        k_ref[:, base + HEAD_DIM:base + 2 * HEAD_DIM] = kr_ref[...]
        v_ref[:, h * HEAD_DIM:(h + 1) * HEAD_DIM] = acc[:, base + HEAD_DIM:base + 2 * HEAD_DIM].astype(v_ref.dtype)


def _kv_up(ckv, w_ukv, kr, g_nope, tm=1024, hps=4):
    m, lat = ckv.shape
    n = w_ukv.shape[1]
    tm = min(tm, m)
    tn = hps * 2 * HEAD_DIM
    return pl.pallas_call(
        _kv_up_body,
        grid=(m // tm, n // tn),
        in_specs=[pl.BlockSpec((tm, lat), lambda i, j: (i, 0)),
                  pl.BlockSpec((lat, tn), lambda i, j: (0, j)),
                  pl.BlockSpec((tm, LANES), lambda i, j: (i, 0)),
                  pl.BlockSpec((1, HEAD_DIM), lambda i, j: (0, 0))],
        out_specs=[pl.BlockSpec((tm, tn), lambda i, j: (i, j)),
                   pl.BlockSpec((tm, tn // 2), lambda i, j: (i, j))],
        out_shape=[jax.ShapeDtypeStruct((m, n), BF16), jax.ShapeDtypeStruct((m, n // 2), BF16)],
        compiler_params=_params("parallel", "parallel"),
    )(ckv, w_ukv, kr, g_nope)


def _q_down_body(a_ref, w_ref, g_ref, o_ref):
    cq = _dot(a_ref[...], w_ref[...].astype(BF16))
    o_ref[...] = _rms(cq, g_ref[...], cq.shape[-1]).astype(o_ref.dtype)


def _q_down(a, w, gain, tm=512):
    m, k = a.shape
    n = w.shape[1]
    tm = min(tm, m)
    return pl.pallas_call(
        _q_down_body,
        grid=(m // tm,),
        in_specs=[pl.BlockSpec((tm, k), lambda i: (i, 0)),
                  pl.BlockSpec((k, n), lambda i: (0, 0)),
                  pl.BlockSpec((1, n), lambda i: (0, 0))],
        out_specs=pl.BlockSpec((tm, n), lambda i: (i, 0)),
        out_shape=jax.ShapeDtypeStruct((m, n), BF16),
        compiler_params=_params("parallel"),
    )(a, w, gain.reshape(1, n))


def _q_up_body(a_ref, w_ref, gn_ref, gr_ref, cos_ref, sa_ref, sb_ref, o_ref, *, scale):
    acc = _dot(a_ref[...], w_ref[...].astype(BF16))
    for h in range(acc.shape[1] // (2 * HEAD_DIM)):
        base = 2 * HEAD_DIM * h
        qn = _rms(acc[:, base:base + HEAD_DIM], gn_ref[...], HEAD_DIM)
        o_ref[:, base:base + HEAD_DIM] = (qn * scale).astype(o_ref.dtype)
        qr = _rms(acc[:, base + HEAD_DIM:base + 2 * HEAD_DIM], gr_ref[...], ROPE_DIM)
        qr = _rope(qr, cos_ref, sa_ref, sb_ref)
        o_ref[:, base + HEAD_DIM:base + 2 * HEAD_DIM] = (qr * scale).astype(o_ref.dtype)


def _q_up(cq, w, g_nope, g_rope, rope_tabs, scale, tm=1024, hps=4):
    m, lat = cq.shape
    n = w.shape[1]
    tm = min(tm, m)
    tn = hps * 2 * HEAD_DIM
    vec = pl.BlockSpec((1, LANES), lambda i, j: (0, 0))
    row = pl.BlockSpec((tm, LANES), lambda i, j: (i, 0))
    return pl.pallas_call(
        functools.partial(_q_up_body, scale=scale),
        grid=(m // tm, n // tn),
        in_specs=[pl.BlockSpec((tm, lat), lambda i, j: (i, 0)),
                  pl.BlockSpec((lat, tn), lambda i, j: (0, j)),
                  vec, vec, row, row, row],
        out_specs=pl.BlockSpec((tm, tn), lambda i, j: (i, j)),
        out_shape=jax.ShapeDtypeStruct((m, n), BF16),
        compiler_params=_params("parallel", "parallel"),
    )(cq, w, g_nope, g_rope, *rope_tabs)


def _attn_body(q_ref, k_ref, v_ref, o_ref, *, tk):
    qi = pl.program_id(1)
    tq = q_ref.shape[0]
    q = q_ref[...]

    def step(start, carry, masked):
        m, l, acc = carry
        k = k_ref[pl.ds(start, tk), :]
        v = v_ref[pl.ds(start, tk), :]
        s = _dot_nt(q, k)
        if masked:
            q_pos = qi * tq + lax.broadcasted_iota(jnp.int32, (tq, tk), 0)
            k_pos = start + lax.broadcasted_iota(jnp.int32, (tq, tk), 1)
            s = jnp.where(k_pos <= q_pos, s, NEG_INF)
        m_new = jnp.maximum(m, jnp.max(s, axis=-1, keepdims=True))
        alpha = jnp.exp(m - m_new)
        p = jnp.exp(s - m_new)
        l = alpha * l + jnp.sum(p, axis=-1, keepdims=True)
        acc = alpha * acc + _dot(p.astype(v.dtype), v)
        return m_new, l, acc

    init = (jnp.full((tq, 1), NEG_INF, F32), jnp.zeros((tq, 1), F32),
            jnp.zeros((tq, v_ref.shape[1]), F32))
    per_q = tq // tk
    carry = lax.fori_loop(
        0, qi * per_q, lambda j, cr: step(pl.multiple_of(j * tk, tk), cr, False), init)
    for d in range(per_q):
        carry = step(pl.multiple_of(qi * tq + d * tk, tk), carry, True)
    _, l, acc = carry
    o_ref[...] = (acc / l).astype(o_ref.dtype)


def _attention(q_cat, k_cat, v, heads, tq=512, tk=512):
    s = q_cat.shape[0]
    tq, tk = min(tq, s), min(tk, s)
    return pl.pallas_call(
        functools.partial(_attn_body, tk=tk),
        grid=(heads, s // tq),
        in_specs=[pl.BlockSpec((tq, 2 * HEAD_DIM), lambda h, i: (i, h)),
                  pl.BlockSpec((s, 2 * HEAD_DIM), lambda h, i: (0, h)),
                  pl.BlockSpec((s, HEAD_DIM), lambda h, i: (0, h))],
        out_specs=pl.BlockSpec((tq, HEAD_DIM), lambda h, i: (i, h)),
        out_shape=jax.ShapeDtypeStruct((s, heads * HEAD_DIM), BF16),
        compiler_params=_params("parallel", "parallel"),
        name="mla_attention",
    )(q_cat, k_cat, v)


def _router_body(h_ref, w_ref, b_ref, info_ref, cnt_ref, run_ref, *, experts):
    i = pl.program_id(0)

    @pl.when(i == 0)
    def _():
        run_ref[...] = jnp.zeros_like(run_ref)

    h = h_ref[...]
    tm = h.shape[0]
    w = w_ref[...]
    w_hi = w.astype(BF16)
    rem = w - w_hi.astype(F32)
    w_mid = rem.astype(BF16)
    w_lo = (rem - w_mid.astype(F32)).astype(BF16)
    logits = _dot(h, w_hi) + _dot(h, w_mid) + _dot(h, w_lo) + b_ref[...]
    lane = lax.broadcasted_iota(jnp.int32, (tm, LANES), 1)
    logits = jnp.where(lane < experts, logits, NEG_INF)
    e = jnp.exp(logits - jnp.max(logits, axis=-1, keepdims=True))
    probs = e / jnp.sum(e, axis=-1, keepdims=True)
    probs = jnp.where(lane < experts, probs, -1.0)
    p1 = jnp.max(probs, axis=-1, keepdims=True)
    i1 = jnp.min(jnp.where(probs == p1, lane, LANES), axis=-1, keepdims=True)
    rest = jnp.where(lane == i1, -1.0, probs)
    p2 = jnp.max(rest, axis=-1, keepdims=True)
    i2 = jnp.min(jnp.where(rest == p2, lane, LANES), axis=-1, keepdims=True)
    total = p1 + p2
    hot1 = (lane == i1).astype(F32)
    hot2 = (lane == i2).astype(F32)
    hot = hot1 + hot2
    r = lax.broadcasted_iota(jnp.int32, (tm, tm), 0)
    c = lax.broadcasted_iota(jnp.int32, (tm, tm), 1)
    before = (c < r).astype(BF16)
    rank = _dot(before, hot.astype(BF16)) + run_ref[...]
    rank1 = jnp.sum(rank * hot1, axis=-1, keepdims=True)
    rank2 = jnp.sum(rank * hot2, axis=-1, keepdims=True)
    vals = (i1.astype(F32), i2.astype(F32), p1 / total, p2 / total, rank1, rank2)
    info = jnp.zeros((tm, LANES), F32)
    for idx, val in enumerate(vals):
        info = jnp.where(lane == idx, val, info)
    info_ref[...] = info
    run_ref[...] += jnp.sum(hot, axis=0, keepdims=True)
    cnt_ref[...] = run_ref[...]


def _router(h, w_pad, b_pad, experts, tm=512):
    s, d = h.shape
    tm = min(tm, s)
    return pl.pallas_call(
        functools.partial(_router_body, experts=experts),
        grid=(s // tm,),
        in_specs=[pl.BlockSpec((tm, d), lambda i: (i, 0)),
                  pl.BlockSpec((d, LANES), lambda i: (0, 0)),
                  pl.BlockSpec((1, LANES), lambda i: (0, 0))],
        out_specs=[pl.BlockSpec((tm, LANES), lambda i: (i, 0)),
                   pl.BlockSpec((1, LANES), lambda i: (0, 0))],
        out_shape=[jax.ShapeDtypeStruct((s, LANES), F32), jax.ShapeDtypeStruct((1, LANES), F32)],
        scratch_shapes=[pltpu.VMEM((1, LANES), F32)],
        compiler_params=_params("arbitrary"),
    )(h, w_pad, b_pad)


def _row_copy(src_ref, src_row, dst_ref, dst_row, sem):
    return pltpu.make_async_copy(src_ref.at[pl.ds(src_row, 1), :], dst_ref.at[pl.ds(dst_row, 1), :], sem)


def _dispatch_body(slot_ref, h_ref, init_ref, o_ref, sem, *, tokens):
    del init_ref
    base = pl.program_id(0) * tokens

    def start(n, carry):
        _row_copy(h_ref, base + n // MOE_TOP_K, o_ref, slot_ref[base * MOE_TOP_K + n], sem).start()
        return carry

    def wait(n, carry):
        _row_copy(h_ref, base + n // MOE_TOP_K, o_ref, slot_ref[base * MOE_TOP_K + n], sem).wait()
        return carry

    lax.fori_loop(0, tokens * MOE_TOP_K, start, 0)
    lax.fori_loop(0, tokens * MOE_TOP_K, wait, 0)


def _dispatch(slots, h_packed, padded_rows, tokens=256):
    s, d = h_packed.shape
    tokens = min(tokens, s)
    init = jnp.zeros((padded_rows, d), h_packed.dtype)
    return pl.pallas_call(
        functools.partial(_dispatch_body, tokens=tokens),
        grid_spec=pltpu.PrefetchScalarGridSpec(
            num_scalar_prefetch=1,
            grid=(s // tokens,),
            in_specs=[pl.BlockSpec(memory_space=pl.ANY), pl.BlockSpec(memory_space=pl.ANY)],
            out_specs=pl.BlockSpec(memory_space=pl.ANY),
            scratch_shapes=[pltpu.SemaphoreType.DMA(())]),
        out_shape=jax.ShapeDtypeStruct((padded_rows, d), h_packed.dtype),
        input_output_aliases={2: 0},
        compiler_params=_params("arbitrary"),
    )(slots, h_packed, init)


def _unpack_rows(words):
    lo = lax.bitcast_convert_type(words << 16, F32)
    hi = lax.bitcast_convert_type(words & jnp.uint32(0xFFFF0000), F32)
    return jnp.concatenate([lo, hi], axis=1).astype(BF16)


def _moe_up_body(tile_ref, col_ref, exp_ref, wcol_ref, valid_ref, a_ref, wg_ref, wu_ref, o_ref):
    del tile_ref, col_ref, exp_ref, wcol_ref
    valid = valid_ref[pl.program_id(0)]

    @pl.when(valid == 1)
    def _():
        a = _unpack_rows(a_ref[...])
        g = _dot(a, wg_ref[...].astype(BF16))
        u = _dot(a, wu_ref[...].astype(BF16))
        o_ref[...] = (_silu(g) * u).astype(o_ref.dtype)

    @pl.when(valid == 0)
    def _():
        o_ref[...] = jnp.zeros_like(o_ref)


def _moe_down_body(tile_ref, col_ref, exp_ref, wcol_ref, valid_ref, a_ref, w_ref, o_ref):
    del tile_ref, col_ref, exp_ref, wcol_ref
    valid = valid_ref[pl.program_id(0)]

    @pl.when(valid == 1)
    def _():
        o_ref[...] = _dot(a_ref[...], w_ref[...].astype(BF16))

    @pl.when(valid == 0)
    def _():
        o_ref[...] = jnp.zeros_like(o_ref)


def _moe_matmul(body, name, sched, a, weights, out_dtype, tm, tn):
    rows = a.shape[0]
    _, k, n = weights[0].shape
    wspec = pl.BlockSpec((None, k, tn), lambda s, tile, col, exp, wcol, valid: (exp[s], 0, wcol[s]))
    return pl.pallas_call(
        body,
        grid_spec=pltpu.PrefetchScalarGridSpec(
            num_scalar_prefetch=5,
            grid=(sched[0].shape[0],),
            in_specs=[pl.BlockSpec((tm, a.shape[1]), lambda s, tile, col, exp, wcol, valid: (tile[s], 0))]
            + [wspec] * len(weights),
            out_specs=pl.BlockSpec((tm, tn), lambda s, tile, col, exp, wcol, valid: (tile[s], col[s]))),
        out_shape=jax.ShapeDtypeStruct((rows, n), out_dtype),
        compiler_params=_params("arbitrary"),
        name=name,
    )(*sched, a, *weights)


def _combine_body(slot_ref, y_ref, x_ref, info_ref, gate_ref, o_ref, buf_ref, sem, *, tokens):
    base = pl.program_id(0) * tokens

    def copy(n):
        return _row_copy(y_ref, slot_ref[base * MOE_TOP_K + n], buf_ref.at[n % MOE_TOP_K], n // MOE_TOP_K, sem)

    def start(n, carry):
        copy(n).start()
        return carry

    def wait(n, carry):
        copy(n).wait()
        return carry

    lax.fori_loop(0, tokens * MOE_TOP_K, start, 0)
    lax.fori_loop(0, tokens * MOE_TOP_K, wait, 0)
    info = info_ref[...]
    mix = info[:, 2:3] * buf_ref[0] + info[:, 3:4] * buf_ref[1]
    o_ref[...] = x_ref[...] + gate_ref[...] * mix


def _combine(slots, y, x, info, gate, tokens=256):
    s, d = x.shape
    tokens = min(tokens, s)
    return pl.pallas_call(
        functools.partial(_combine_body, tokens=tokens),
        grid_spec=pltpu.PrefetchScalarGridSpec(
            num_scalar_prefetch=1,
            grid=(s // tokens,),
            in_specs=[pl.BlockSpec(memory_space=pl.ANY),
                      pl.BlockSpec((tokens, d), lambda i, slot: (i, 0)),
                      pl.BlockSpec((tokens, LANES), lambda i, slot: (i, 0)),
                      pl.BlockSpec((1, d), lambda i, slot: (0, 0))],
            out_specs=pl.BlockSpec((tokens, d), lambda i, slot: (i, 0)),
            scratch_shapes=[pltpu.VMEM((MOE_TOP_K, tokens, d), F32), pltpu.SemaphoreType.DMA(())]),
        out_shape=jax.ShapeDtypeStruct((s, d), F32),
        compiler_params=_params("arbitrary"),
    )(slots, y, x, info, gate.reshape(1, d))


def _moe_schedule(info, counts, experts, tm, n_col):
    s = info.shape[0]
    max_tiles = (s * MOE_TOP_K) // tm + experts
    ids = info[:, 0:MOE_TOP_K].astype(jnp.int32)
    ranks = info[:, 4:4 + MOE_TOP_K].astype(jnp.int32)
    cnt = counts[0, :experts].astype(jnp.int32)
    tiles = (cnt + tm - 1) // tm
    tile_end = jnp.cumsum(tiles)
    tile_start = tile_end - tiles
    slots = (tile_start * tm)[ids] + ranks
    step_end = tile_end * n_col
    total = step_end[-1]
    steps = jnp.arange(max_tiles * n_col, dtype=jnp.int32)
    valid = steps < total
    st = jnp.minimum(steps, total - 1)
    exp = jnp.searchsorted(step_end, st, side="right").astype(jnp.int32)
    local = st - (step_end - tiles * n_col)[exp]
    wcol = local // tiles[exp]
    spare = steps - total
    col = jnp.where(valid, wcol, spare % n_col)
    tile = jnp.where(valid, tile_start[exp] + local % tiles[exp], tile_end[-1] + spare // n_col)
    sched = tuple(v.astype(jnp.int32) for v in (tile, col, exp, wcol, valid))
    return slots.reshape(-1).astype(jnp.int32), sched, max_tiles * tm


def _rope_tables(positions):
    half = ROPE_DIM // 2
    inv_freq = ROPE_THETA ** (-jnp.arange(0, ROPE_DIM, 2, dtype=F32) / ROPE_DIM)
    ang = positions.astype(F32)[:, None] * inv_freq
    cos, sin = jnp.cos(ang), jnp.sin(ang)
    zero = jnp.zeros_like(cos)
    pad = jnp.zeros((cos.shape[0], LANES - ROPE_DIM), F32)
    return (jnp.concatenate([cos, cos, pad], axis=1),
            jnp.concatenate([-sin, zero, pad], axis=1),
            jnp.concatenate([zero, sin, pad], axis=1))


def _pad_lanes(v):
    return jnp.pad(v, ((0, 0), (0, LANES - v.shape[1])))


def kernel(x, c, positions, ada_w, ada_b, norm_mix, norm_ffn, gdn_w_in, gdn_conv, gdn_a_log, gdn_dt_bias, gdn_out_norm, gdn_w_out, kv_norm, w_dkv, kv_latent_norm, w_ukv, k_nope_norm, k_rope_norm, mla_w_dq, mla_q_latent_norm, mla_w_uq, mla_q_nope_norm, mla_q_rope_norm, mla_w_out, ffn_w_gate, ffn_w_up, ffn_w_down, router_w, router_b, moe_w_gate, moe_w_up, moe_w_down):
    assert x.shape[0] == 1 and ada_w.shape[0] == 2
    _, s, d = x.shape
    heads = d // HEAD_DIM
    width = heads * HEAD_DIM
    x0 = x[0]

    mod = _ada_mod(c, ada_w, ada_b).reshape(2, ADA_CHUNKS, d)
    rope_tabs = _rope_tables(positions[0])

    sh_m, sc_m, g_m, sh_f, sc_f, g_f = (mod[0, n] for n in range(ADA_CHUNKS))
    h = _norm_mod(x0, norm_mix[0], sc_m, sh_m, name="norm_mix0")
    w_in = gdn_w_in[0].astype(BF16)
    proj = _matmul(h, w_in[:, :4 * width], name="gdn_in_proj", out_dtype=BF16)
    b_raw = _matmul(h, _pad_lanes(w_in[:, 4 * width:4 * width + heads]), name="gdn_beta_proj", out_dtype=F32)
    a_raw = _matmul(h, _pad_lanes(w_in[:, 4 * width + heads:]), name="gdn_decay_proj", out_dtype=F32)
    qkv = _gdn_conv(proj, gdn_conv[0], width)
    gcb, bb, gct = _gdn_gate(b_raw, a_raw, _pad_lanes(gdn_a_log[0][None]), _pad_lanes(gdn_dt_bias[0][None]), width)
    o = _gdn_chunk(qkv, proj, gcb, bb, gct[:heads], gdn_out_norm[0], heads)
    x1 = _matmul(o, gdn_w_out[0].astype(BF16), name="gdn_out_proj", out_dtype=F32, res=x0, gate=g_m)

    h = _norm_mod(x1, norm_ffn[0], sc_f, sh_f, name="norm_ffn0")
    hid = _swiglu_up(h, ffn_w_gate[0].astype(BF16), ffn_w_up[0].astype(BF16))
    ffn_dim = hid.shape[1]
    x2 = _matmul_ktiled_res(hid, ffn_w_down[0].astype(BF16), x1, g_f, tk=ffn_dim // 2)

    lat = kv_latent_norm.shape[0]
    zeros = jnp.zeros((d,), F32)
    h_kv = _norm_mod(x2, kv_norm, zeros, zeros, name="norm_kv", modulate=False)
    w_dkv16 = w_dkv.astype(BF16)
    ckv, k_rope = _kv_down(h_kv, w_dkv16[:, :lat], _pad_lanes(w_dkv16[:, lat:]), kv_latent_norm[None],
                           _pad_lanes(k_rope_norm[None]), rope_tabs)
    k_cat, v = _kv_up(ckv, w_ukv.astype(BF16), k_rope, k_nope_norm[None])

    sh_m, sc_m, g_m, sh_f, sc_f, g_f = (mod[1, n] for n in range(ADA_CHUNKS))
    h = _norm_mod(x2, norm_mix[1], sc_m, sh_m, name="norm_mix1")
    cq = _q_down(h, mla_w_dq[0].astype(BF16), mla_q_latent_norm[0])
    q_lora = cq.shape[1]
    w_uq = mla_w_uq[0].astype(BF16).reshape(q_lora, heads, HEAD_DIM + ROPE_DIM)
    w_uq = jnp.pad(w_uq, ((0, 0), (0, 0), (0, 2 * HEAD_DIM - HEAD_DIM - ROPE_DIM))).reshape(q_lora, heads * 2 * HEAD_DIM)
    q_cat = _q_up(cq, w_uq, mla_q_nope_norm[0][None], _pad_lanes(mla_q_rope_norm[0][None]), rope_tabs,
                  (HEAD_DIM + ROPE_DIM) ** -0.5)
    o = _attention(q_cat, k_cat, v, heads)
    x3 = _matmul(o, mla_w_out[0].astype(BF16), name="mla_out_proj", out_dtype=F32, res=x2, gate=g_m)

    experts = router_w.shape[-1]
    moe_tm, moe_tn = 512, 512
    assert moe_w_gate.shape[-1] == d
    h, h_packed = _norm_mod(x3, norm_ffn[1], sc_f, sh_f, name="norm_ffn1", packed=True)
    info, counts = _router(h, _pad_lanes(router_w[0]), _pad_lanes(router_b[0][None]), experts)
    slots, sched, padded_rows = _moe_schedule(info, counts, experts, moe_tm, d // moe_tn)
    sorted_h = _dispatch(slots, h_packed, padded_rows)
    hid = _moe_matmul(_moe_up_body, "moe_up", sched, sorted_h,
                      (moe_w_gate[0].astype(BF16), moe_w_up[0].astype(BF16)), BF16, moe_tm, moe_tn)
    y = _moe_matmul(_moe_down_body, "moe_down", sched, hid, (moe_w_down[0].astype(BF16),), F32, moe_tm, moe_tn)
    out = _combine(slots, y, x3, info, g_f)
    return out[None]
```

```python
import functools
import math

import jax
import jax.numpy as jnp
from jax import lax
from jax.experimental import pallas as pl
from jax.experimental.pallas import tpu as pltpu

F32 = jnp.float32
BF16 = jnp.bfloat16
HIGHEST = lax.Precision.HIGHEST

RMS_EPS = 1e-6
L2_EPS = 1e-6
NEG_INF = -1e30
ROPE_THETA = 10000.0

LANES = 128
HEAD_DIM = 128
ROPE_DIM = 64
GDN_CHUNK = 64
GDN_CONV = 4
MOE_TOP_K = 2
ADA_CHUNKS = 6
HALO_ROWS = 16
VMEM_LIMIT_BYTES = 56 * 1024 * 1024


def _params(*semantics):
    return pltpu.CompilerParams(dimension_semantics=semantics, vmem_limit_bytes=VMEM_LIMIT_BYTES)


def _dot(a, b):
    return jnp.dot(a, b, preferred_element_type=F32)


def _dot_nt(a, b):
    return lax.dot_general(a, b, (((1,), (1,)), ((), ())), preferred_element_type=F32)


def _dot_tn(a, b):
    return lax.dot_general(a, b, (((0,), (0,)), ((), ())), preferred_element_type=F32)


def _silu(x):
    return x * jax.nn.sigmoid(x)


def _rms(x, gain, n):
    ms = jnp.sum(x * x, axis=-1, keepdims=True) * (1.0 / n)
    return x * lax.rsqrt(ms + RMS_EPS) * gain


def _ada_body(c_ref, w_ref, b_ref, o_ref):
    cs = _silu(c_ref[...])
    for j in range(o_ref.shape[-1] // LANES):
        sl = slice(j * LANES, (j + 1) * LANES)
        s = jnp.sum(w_ref[0, :, sl] * cs, axis=0, keepdims=True)
        o_ref[0, :, sl] = s + b_ref[0, :, sl]


def _ada_mod(c, ada_w, ada_b, tn=1024):
    depth, d, n = ada_w.shape
    c_b = jnp.broadcast_to(c.reshape(d, 1), (d, LANES))
    return pl.pallas_call(
        _ada_body,
        grid=(depth, n // tn),
        in_specs=[pl.BlockSpec((d, LANES), lambda l, j: (0, 0)),
                  pl.BlockSpec((1, d, tn), lambda l, j: (l, 0, j)),
                  pl.BlockSpec((1, 1, tn), lambda l, j: (l, 0, j))],
        out_specs=pl.BlockSpec((1, 1, tn), lambda l, j: (l, 0, j)),
        out_shape=jax.ShapeDtypeStruct((depth, 1, n), F32),
        compiler_params=_params("parallel", "parallel"),
    )(c_b, ada_w, ada_b.reshape(depth, 1, n))


def _norm_body(x_ref, g_ref, sc_ref, sh_ref, *o_refs, modulate, packed):
    x = x_ref[...]
    d = x.shape[-1]
    y = _rms(x, g_ref[...], d)
    if modulate:
        y = y * (1.0 + sc_ref[...]) + sh_ref[...]
    yb = y.astype(BF16)
    o_refs[0][...] = yb
    if packed:
        bits = lax.bitcast_convert_type(yb.astype(F32), jnp.uint32)
        lo = bits[:, :d // 2] >> 16
        hi = bits[:, d // 2:] & jnp.uint32(0xFFFF0000)
        o_refs[1][...] = hi | lo


def _norm_mod(x, gain, scale, shift, *, name, modulate=True, packed=False, tr=256):
    s, d = x.shape
    tr = min(tr, s)
    vec = pl.BlockSpec((1, d), lambda i: (0, 0))
    out_shape = [jax.ShapeDtypeStruct((s, d), BF16)]
    out_specs = [pl.BlockSpec((tr, d), lambda i: (i, 0))]
    if packed:
        out_shape.append(jax.ShapeDtypeStruct((s, d // 2), jnp.uint32))
        out_specs.append(pl.BlockSpec((tr, d // 2), lambda i: (i, 0)))
    outs = pl.pallas_call(
        functools.partial(_norm_body, modulate=modulate, packed=packed),
        grid=(s // tr,),
        in_specs=[pl.BlockSpec((tr, d), lambda i: (i, 0)), vec, vec, vec],
        out_specs=out_specs,
        out_shape=out_shape,
        compiler_params=_params("parallel"),
        name=name,
    )(x, gain.reshape(1, d), scale.reshape(1, d), shift.reshape(1, d))
    return outs if packed else outs[0]


def _mm_body(a_ref, w_ref, o_ref):
    o_ref[...] = _dot(a_ref[...], w_ref[...].astype(BF16)).astype(o_ref.dtype)


def _mm_res_body(a_ref, w_ref, r_ref, g_ref, o_ref):
    o_ref[...] = r_ref[...] + g_ref[...] * _dot(a_ref[...], w_ref[...].astype(BF16))


def _matmul(a, w, *, name, out_dtype, tm=1024, tn=512, res=None, gate=None):
    m, k = a.shape
    n = w.shape[1]
    tm, tn = min(tm, m), min(tn, n)
    in_specs = [pl.BlockSpec((tm, k), lambda i, j: (i, 0)),
                pl.BlockSpec((k, tn), lambda i, j: (0, j))]
    args = [a, w]
    body = _mm_body
    if res is not None:
        in_specs += [pl.BlockSpec((tm, tn), lambda i, j: (i, j)),
                     pl.BlockSpec((1, tn), lambda i, j: (0, j))]
        args += [res, gate.reshape(1, n)]
        body = _mm_res_body
    return pl.pallas_call(
        body,
        grid=(m // tm, n // tn),
        in_specs=in_specs,
        out_specs=pl.BlockSpec((tm, tn), lambda i, j: (i, j)),
        out_shape=jax.ShapeDtypeStruct((m, n), out_dtype),
        compiler_params=_params("parallel", "parallel"),
        name=name,
    )(*args)


def _mm_ktiled_res_body(a_ref, w_ref, r_ref, g_ref, o_ref, acc_ref):
    kk = pl.program_id(2)

    @pl.when(kk == 0)
    def _():
        acc_ref[...] = jnp.zeros_like(acc_ref)

    acc_ref[...] += _dot(a_ref[...], w_ref[...].astype(BF16))

    @pl.when(kk == pl.num_programs(2) - 1)
    def _():
        o_ref[...] = r_ref[...] + g_ref[...] * acc_ref[...]


def _matmul_ktiled_res(a, w, res, gate, *, tm=512, tn=1024, tk):
    m, k = a.shape
    n = w.shape[1]
    tm, tn = min(tm, m), min(tn, n)
    return pl.pallas_call(
        _mm_ktiled_res_body,
        grid=(m // tm, n // tn, k // tk),
        in_specs=[pl.BlockSpec((tm, tk), lambda i, j, kk: (i, kk)),
                  pl.BlockSpec((tk, tn), lambda i, j, kk: (kk, j)),
                  pl.BlockSpec((tm, tn), lambda i, j, kk: (i, j)),
                  pl.BlockSpec((1, tn), lambda i, j, kk: (0, j))],
        out_specs=pl.BlockSpec((tm, tn), lambda i, j, kk: (i, j)),
        out_shape=jax.ShapeDtypeStruct((m, n), F32),
        scratch_shapes=[pltpu.VMEM((tm, tn), F32)],
        compiler_params=_params("parallel", "parallel", "arbitrary"),
    )(a, w, res, gate.reshape(1, n))


def _swiglu_up_body(a_ref, wg_ref, wu_ref, o_ref):
    a = a_ref[...]
    g = _dot(a, wg_ref[...].astype(BF16))
    u = _dot(a, wu_ref[...].astype(BF16))
    o_ref[...] = (_silu(g) * u).astype(o_ref.dtype)


def _swiglu_up(a, wg, wu, *, tm=1024, tn=256):
    m, k = a.shape
    n = wg.shape[1]
    tm = min(tm, m)
    wspec = pl.BlockSpec((k, tn), lambda i, j: (0, j))
    return pl.pallas_call(
        _swiglu_up_body,
        grid=(m // tm, n // tn),
        in_specs=[pl.BlockSpec((tm, k), lambda i, j: (i, 0)), wspec, wspec],
        out_specs=pl.BlockSpec((tm, tn), lambda i, j: (i, j)),
        out_shape=jax.ShapeDtypeStruct((m, n), BF16),
        compiler_params=_params("parallel", "parallel"),
    )(a, wg, wu)


def _gdn_conv_body(halo_ref, x_ref, w_ref, o_ref, buf_ref):
    i = pl.program_id(0)
    which = pl.program_id(1)
    tr = x_ref.shape[0]
    halo = halo_ref[...].astype(F32)
    buf_ref[0:HALO_ROWS, :] = jnp.where(i > 0, halo, 0.0)
    buf_ref[HALO_ROWS:HALO_ROWS + tr, :] = x_ref[...].astype(F32)
    for h in range(x_ref.shape[1] // HEAD_DIM):
        sl = slice(h * HEAD_DIM, (h + 1) * HEAD_DIM)
        y = None
        for j in range(GDN_CONV):
            start = HALO_ROWS - (GDN_CONV - 1) + j
            term = buf_ref[start:start + tr, sl] * w_ref[j:j + 1, sl]
            y = term if y is None else y + term
        y = _silu(y)
        ss = jnp.sum(y * y, axis=-1, keepdims=True)
        yn = y * lax.rsqrt(ss + L2_EPS)
        o_ref[:, sl] = jnp.where(which < 2, yn, y).astype(o_ref.dtype)


def _gdn_conv(proj, conv_w, width, tr=256):
    s = proj.shape[0]
    tr = min(tr, s)
    hb = tr // HALO_ROWS
    return pl.pallas_call(
        _gdn_conv_body,
        grid=(s // tr, 3),
        in_specs=[pl.BlockSpec((HALO_ROWS, width), lambda i, c: (jnp.maximum(i * hb - 1, 0), c)),
                  pl.BlockSpec((tr, width), lambda i, c: (i, c)),
                  pl.BlockSpec((GDN_CONV, width), lambda i, c: (0, c))],
        out_specs=pl.BlockSpec((tr, width), lambda i, c: (i, c)),
        out_shape=jax.ShapeDtypeStruct((s, 3 * width), BF16),
        scratch_shapes=[pltpu.VMEM((HALO_ROWS + tr, width), F32)],
        compiler_params=_params("parallel", "parallel"),
    )(proj, proj, conv_w)


def _gdn_gate_body(b_ref, a_ref, alog_ref, dtb_ref, gcb_ref, bb_ref, gct_ref):
    tr = b_ref.shape[0]
    beta = jax.nn.sigmoid(b_ref[...])
    x = a_ref[...] + dtb_ref[...]
    softplus = jnp.maximum(x, 0.0) + jnp.log1p(jnp.exp(-jnp.abs(x)))
    g = -jnp.exp(alog_ref[...]) * softplus
    r = lax.broadcasted_iota(jnp.int32, (tr, tr), 0)
    c = lax.broadcasted_iota(jnp.int32, (tr, tr), 1)
    shift = int(math.log2(GDN_CHUNK))
    tri = ((r >> shift == c >> shift) & (c <= r)).astype(F32)
    gc = jnp.dot(tri, g, precision=HIGHEST, preferred_element_type=F32)
    width = gcb_ref.shape[1]
    er = lax.broadcasted_iota(jnp.int32, (LANES, width), 0)
    ec = lax.broadcasted_iota(jnp.int32, (LANES, width), 1)
    expand = (ec >> int(math.log2(HEAD_DIM)) == er).astype(F32)
    gcb_ref[...] = jnp.dot(gc, expand, precision=HIGHEST, preferred_element_type=F32)
    bb_ref[...] = jnp.dot(beta, expand, precision=HIGHEST, preferred_element_type=F32)
    ir = lax.broadcasted_iota(jnp.int32, (LANES, LANES), 0)
    ic = lax.broadcasted_iota(jnp.int32, (LANES, LANES), 1)
    eye = (ir == ic).astype(F32)
    gct_ref[...] = lax.dot_general(eye, gc, (((1,), (1,)), ((), ())), precision=HIGHEST,
                                   preferred_element_type=F32)


def _gdn_gate(b_raw, a_raw, a_log, dt_bias, width, tr=512):
    s = b_raw.shape[0]
    tr = min(tr, s)
    row = pl.BlockSpec((tr, LANES), lambda i: (i, 0))
    vec = pl.BlockSpec((1, LANES), lambda i: (0, 0))
    wide = pl.BlockSpec((tr, width), lambda i: (i, 0))
    return pl.pallas_call(
        _gdn_gate_body,
        grid=(s // tr,),
        in_specs=[row, row, vec, vec],
        out_specs=[wide, wide, pl.BlockSpec((LANES, tr), lambda i: (0, i))],
        out_shape=[jax.ShapeDtypeStruct((s, width), F32), jax.ShapeDtypeStruct((s, width), F32),
                   jax.ShapeDtypeStruct((LANES, s), F32)],
        compiler_params=_params("parallel"),
    )(b_raw, a_raw, a_log, dt_bias)


def _gdn_chunk_body(q_ref, k_ref, v_ref, gcb_ref, bb_ref, gct_ref, z_ref, gain_ref, o_ref, state_ref):
    @pl.when(pl.program_id(1) == 0)
    def _():
        state_ref[...] = jnp.zeros_like(state_ref)

    cs = GDN_CHUNK
    n_heads = state_ref.shape[0]
    n_chunks = q_ref.shape[0] // cs
    probs = [(j, n) for j in range(n_heads) for n in range(n_chunks)]

    def tile(ref, j, n):
        return ref[n * cs:(n + 1) * cs, j * HEAD_DIM:(j + 1) * HEAD_DIM]

    r = lax.broadcasted_iota(jnp.int32, (cs, cs), 0)
    c = lax.broadcasted_iota(jnp.int32, (cs, cs), 1)
    causal = c <= r
    strict = c < r
    eye = (r == c).astype(F32)

    k = [tile(k_ref, j, n).astype(F32) for j, n in probs]
    k16 = [x.astype(BF16) for x in k]
    gc = [tile(gcb_ref, j, n) for j, n in probs]
    beta = [tile(bb_ref, j, n) for j, n in probs]
    kb = [x * b for x, b in zip(k, beta)]
    decay = []
    for (j, n), g in zip(probs, gc):
        gdiff = g[:, :cs] - gct_ref[j, :, n * cs:(n + 1) * cs]
        decay.append(jnp.where(causal, jnp.exp(jnp.where(causal, gdiff, 0.0)), 0.0))
    q = [tile(q_ref, j, n).astype(F32) * (HEAD_DIM ** -0.5) for j, n in probs]
    kq = [_dot_nt(jnp.concatenate([x.astype(BF16), y.astype(BF16)], axis=0), z) for x, y, z in zip(kb, q, k16)]
    p = [jnp.where(strict, -(x[:cs] * d), 0.0) for x, d in zip(kq, decay)]
    inv = [eye + x for x in p]
    for _ in range(int(math.log2(cs)) - 1):
        p16 = [x.astype(BF16) for x in p]
        p = [_dot(x, x) for x in p16]
        inv = [x + _dot(x.astype(BF16), y.astype(BF16)) for x, y in zip(inv, p)]
    t16 = [x.astype(BF16) for x in inv]
    uw = [_dot(t, jnp.concatenate([(tile(v_ref, j, n).astype(F32) * b).astype(BF16),
                                   (x * jnp.exp(g)).astype(BF16)], axis=1))
          for t, (j, n), b, x, g in zip(t16, probs, beta, kb, gc)]
    qk = [(x[cs:] * d).astype(BF16) for x, d in zip(kq, decay)]
    wq = [jnp.concatenate([x[:, HEAD_DIM:].astype(BF16), (y * jnp.exp(g)).astype(BF16)], axis=0)
          for x, y, g in zip(uw, q, gc)]
    g_last = [g[cs - 1:cs, :] for g in gc]
    k_end = [(x * jnp.exp(gl - g)).astype(BF16) for x, gl, g in zip(k, g_last, gc)]

    state = [state_ref[j] for j in range(n_heads)]
    for n in range(n_chunks):
        idx = [j * n_chunks + n for j in range(n_heads)]
        ws = [_dot(wq[i], s.astype(BF16)) for i, s in zip(idx, state)]
        v16 = [(uw[i][:, :HEAD_DIM] - x[:cs]).astype(BF16) for i, x in zip(idx, ws)]
        o = [x[cs:] + _dot(qk[i], v) for i, x, v in zip(idx, ws, v16)]
        state = [s * jnp.exp(g_last[i]) + _dot_tn(k_end[i], v) for i, s, v in zip(idx, state, v16)]
        for j in range(n_heads):
            z = tile(z_ref, j, n).astype(F32)
            o_ref[n * cs:(n + 1) * cs, j * HEAD_DIM:(j + 1) * HEAD_DIM] = (
                _rms(o[j], gain_ref[...], HEAD_DIM) * _silu(z)).astype(o_ref.dtype)
    for j in range(n_heads):
        state_ref[j] = state[j]


def _gdn_chunk(qkv, proj, gcb, bb, gct, out_norm, heads, rows=256, group=8):
    s = qkv.shape[0]
    rows = min(rows, s)
    groups = heads // group
    blk = lambda off: pl.BlockSpec((rows, group * HEAD_DIM), lambda h, r: (r, off + h))
    return pl.pallas_call(
        _gdn_chunk_body,
        grid=(groups, s // rows),
        in_specs=[blk(0), blk(groups), blk(2 * groups), blk(0), blk(0),
                  pl.BlockSpec((group, 1, rows), lambda h, r: (h, 0, r)),
                  blk(3 * groups),
                  pl.BlockSpec((1, HEAD_DIM), lambda h, r: (0, 0))],
        out_specs=blk(0),
        out_shape=jax.ShapeDtypeStruct((s, heads * HEAD_DIM), BF16),
        scratch_shapes=[pltpu.VMEM((group, HEAD_DIM, HEAD_DIM), F32)],
        compiler_params=_params("parallel", "arbitrary"),
        name="gdn_chunk",
    )(qkv, qkv, qkv, gcb, bb, gct.reshape(gct.shape[0], 1, s), proj, out_norm.reshape(1, HEAD_DIM))


def _rope(x, cos_ref, sin_a_ref, sin_b_ref):
    half = ROPE_DIM // 2
    return (x * cos_ref[...] + pltpu.roll(x, LANES - half, 1) * sin_a_ref[...]
            + pltpu.roll(x, half, 1) * sin_b_ref[...])


def _kv_down_body(a_ref, w1_ref, w2_ref, g1_ref, g2_ref, cos_ref, sa_ref, sb_ref, ckv_ref, kr_ref):
    a = a_ref[...]
    ckv = _dot(a, w1_ref[...].astype(BF16))
    ckv_ref[...] = _rms(ckv, g1_ref[...], ckv.shape[-1]).astype(ckv_ref.dtype)
    kr = _rms(_dot(a, w2_ref[...].astype(BF16)), g2_ref[...], ROPE_DIM)
    kr_ref[...] = _rope(kr, cos_ref, sa_ref, sb_ref).astype(kr_ref.dtype)


def _kv_down(a, w_c, w_r, g_c, g_r, rope_tabs, tm=512):
    m, k = a.shape
    lat = w_c.shape[1]
    tm = min(tm, m)
    full = lambda shp: pl.BlockSpec(shp, lambda i: (0, 0))
    row = lambda w: pl.BlockSpec((tm, w), lambda i: (i, 0))
    return pl.pallas_call(
        _kv_down_body,
        grid=(m // tm,),
        in_specs=[row(k), full((k, lat)), full((k, LANES)), full((1, lat)), full((1, LANES)),
                  row(LANES), row(LANES), row(LANES)],
        out_specs=[row(lat), row(LANES)],
        out_shape=[jax.ShapeDtypeStruct((m, lat), BF16), jax.ShapeDtypeStruct((m, LANES), BF16)],
        compiler_params=_params("parallel"),
    )(a, w_c, w_r, g_c, g_r, *rope_tabs)


def _kv_up_body(a_ref, w_ref, kr_ref, g_ref, k_ref, v_ref):
    acc = _dot(a_ref[...], w_ref[...].astype(BF16))
    for h in range(acc.shape[1] // (2 * HEAD_DIM)):
        base = 2 * HEAD_DIM * h
        kn = acc[:, base:base + HEAD_DIM]
        k_ref[:, base:base + HEAD_DIM] = _rms(kn, g_ref[...], HEAD_DIM).astype(k_ref.dtype)
        k_ref[:, base + HEAD_DIM:base + 2 * HEAD_DIM] = kr_ref[...]
        v_ref[:, h * HEAD_DIM:(h + 1) * HEAD_DIM] = acc[:, base + HEAD_DIM:base + 2 * HEAD_DIM].astype(v_ref.dtype)


def _kv_up(ckv, w_ukv, kr, g_nope, tm=1024, hps=4):
    m, lat = ckv.shape
    n = w_ukv.shape[1]
    tm = min(tm, m)
    tn = hps * 2 * HEAD_DIM
    return pl.pallas_call(
        _kv_up_body,
        grid=(m // tm, n // tn),
        in_specs=[pl.BlockSpec((tm, lat), lambda i, j: (i, 0)),
                  pl.BlockSpec((lat, tn), lambda i, j: (0, j)),
                  pl.BlockSpec((tm, LANES), lambda i, j: (i, 0)),
                  pl.BlockSpec((1, HEAD_DIM), lambda i, j: (0, 0))],
        out_specs=[pl.BlockSpec((tm, tn), lambda i, j: (i, j)),
                   pl.BlockSpec((tm, tn // 2), lambda i, j: (i, j))],
        out_shape=[jax.ShapeDtypeStruct((m, n), BF16), jax.ShapeDtypeStruct((m, n // 2), BF16)],
        compiler_params=_params("parallel", "parallel"),
    )(ckv, w_ukv, kr, g_nope)


def _q_down_body(a_ref, w_ref, g_ref, o_ref):
    cq = _dot(a_ref[...], w_ref[...].astype(BF16))
    o_ref[...] = _rms(cq, g_ref[...], cq.shape[-1]).astype(o_ref.dtype)


def _q_down(a, w, gain, tm=512):
    m, k = a.shape
    n = w.shape[1]
    tm = min(tm, m)
    return pl.pallas_call(
        _q_down_body,
        grid=(m // tm,),
        in_specs=[pl.BlockSpec((tm, k), lambda i: (i, 0)),
                  pl.BlockSpec((k, n), lambda i: (0, 0)),
                  pl.BlockSpec((1, n), lambda i: (0, 0))],
        out_specs=pl.BlockSpec((tm, n), lambda i: (i, 0)),
        out_shape=jax.ShapeDtypeStruct((m, n), BF16),
        compiler_params=_params("parallel"),
    )(a, w, gain.reshape(1, n))


def _q_up_body(a_ref, w_ref, gn_ref, gr_ref, cos_ref, sa_ref, sb_ref, o_ref, *, scale):
    acc = _dot(a_ref[...], w_ref[...].astype(BF16))
    for h in range(acc.shape[1] // (2 * HEAD_DIM)):
        base = 2 * HEAD_DIM * h
        qn = _rms(acc[:, base:base + HEAD_DIM], gn_ref[...], HEAD_DIM)
        o_ref[:, base:base + HEAD_DIM] = (qn * scale).astype(o_ref.dtype)
        qr = _rms(acc[:, base + HEAD_DIM:base + 2 * HEAD_DIM], gr_ref[...], ROPE_DIM)
        qr = _rope(qr, cos_ref, sa_ref, sb_ref)
        o_ref[:, base + HEAD_DIM:base + 2 * HEAD_DIM] = (qr * scale).astype(o_ref.dtype)


def _q_up(cq, w, g_nope, g_rope, rope_tabs, scale, tm=1024, hps=4):
    m, lat = cq.shape
    n = w.shape[1]
    tm = min(tm, m)
    tn = hps * 2 * HEAD_DIM
    vec = pl.BlockSpec((1, LANES), lambda i, j: (0, 0))
    row = pl.BlockSpec((tm, LANES), lambda i, j: (i, 0))
    return pl.pallas_call(
        functools.partial(_q_up_body, scale=scale),
        grid=(m // tm, n // tn),
        in_specs=[pl.BlockSpec((tm, lat), lambda i, j: (i, 0)),
                  pl.BlockSpec((lat, tn), lambda i, j: (0, j)),
                  vec, vec, row, row, row],
        out_specs=pl.BlockSpec((tm, tn), lambda i, j: (i, j)),
        out_shape=jax.ShapeDtypeStruct((m, n), BF16),
        compiler_params=_params("parallel", "parallel"),
    )(cq, w, g_nope, g_rope, *rope_tabs)


def _attn_body(q_ref, k_ref, v_ref, o_ref, s_ref):
    qi = pl.program_id(1)
    t = q_ref.shape[0]
    q = q_ref[...]

    def scores(j, slot):
        s_ref[slot] = _dot_nt(q, k_ref[pl.ds(pl.multiple_of(j * t, t), t), :])

    def update(j, slot, carry, masked):
        m, l, acc = carry
        s = s_ref[slot]
        if masked:
            row = lax.broadcasted_iota(jnp.int32, (t, t), 0)
            col = lax.broadcasted_iota(jnp.int32, (t, t), 1)
            s = jnp.where(col <= row, s, NEG_INF)
        m_new = jnp.maximum(m, jnp.max(s, axis=-1, keepdims=True))
        alpha = jnp.exp2(m - m_new)
        p = jnp.exp2(s - m_new)
        l = alpha * l + jnp.sum(p, axis=-1, keepdims=True)
        v = v_ref[pl.ds(pl.multiple_of(j * t, t), t), :]
        acc = alpha * acc + _dot(p.astype(v.dtype), v)
        return m_new, l, acc

    def pair(i, carry):
        scores(2 * i + 1, 1)
        carry = update(2 * i, 0, carry, False)
        scores(2 * i + 2, 0)
        return update(2 * i + 1, 1, carry, False)

    def odd_tail(carry):
        scores(qi, 1)
        return update(qi, 1, update(qi - 1, 0, carry, False), True)

    def even_tail(carry):
        return update(qi, 0, carry, True)

    scores(0, 0)
    init = (jnp.full((t, 1), NEG_INF, F32), jnp.zeros((t, 1), F32), jnp.zeros((t, v_ref.shape[1]), F32))
    carry = lax.fori_loop(0, qi // 2, pair, init)
    _, l, acc = lax.cond(qi % 2 == 1, odd_tail, even_tail, carry)
    o_ref[...] = (acc / l).astype(o_ref.dtype)


def _attention(q_cat, k_cat, v, heads, t=512):
    s = q_cat.shape[0]
    t = min(t, s)
    return pl.pallas_call(
        _attn_body,
        grid=(heads, s // t),
        in_specs=[pl.BlockSpec((t, 2 * HEAD_DIM), lambda h, i: (i, h)),
                  pl.BlockSpec((s, 2 * HEAD_DIM), lambda h, i: (0, h)),
                  pl.BlockSpec((s, HEAD_DIM), lambda h, i: (0, h))],
        out_specs=pl.BlockSpec((t, HEAD_DIM), lambda h, i: (i, h)),
        out_shape=jax.ShapeDtypeStruct((s, heads * HEAD_DIM), BF16),
        scratch_shapes=[pltpu.VMEM((2, t, t), F32)],
        compiler_params=_params("parallel", "parallel"),
        name="mla_attention",
    )(q_cat, k_cat, v)


def _router_body(h_ref, w_ref, b_ref, info_ref, cnt_ref, run_ref, *, experts):
    i = pl.program_id(0)

    @pl.when(i == 0)
    def _():
        run_ref[...] = jnp.zeros_like(run_ref)

    h = h_ref[...]
    tm = h.shape[0]
    w = w_ref[...]
    w_hi = w.astype(BF16)
    rem = w - w_hi.astype(F32)
    w_mid = rem.astype(BF16)
    w_lo = (rem - w_mid.astype(F32)).astype(BF16)
    logits = _dot(h, w_hi) + _dot(h, w_mid) + _dot(h, w_lo) + b_ref[...]
    lane = lax.broadcasted_iota(jnp.int32, (tm, LANES), 1)
    logits = jnp.where(lane < experts, logits, NEG_INF)
    e = jnp.exp(logits - jnp.max(logits, axis=-1, keepdims=True))
    probs = e / jnp.sum(e, axis=-1, keepdims=True)
    probs = jnp.where(lane < experts, probs, -1.0)
    p1 = jnp.max(probs, axis=-1, keepdims=True)
    i1 = jnp.min(jnp.where(probs == p1, lane, LANES), axis=-1, keepdims=True)
    rest = jnp.where(lane == i1, -1.0, probs)
    p2 = jnp.max(rest, axis=-1, keepdims=True)
    i2 = jnp.min(jnp.where(rest == p2, lane, LANES), axis=-1, keepdims=True)
    total = p1 + p2
    hot1 = (lane == i1).astype(F32)
    hot2 = (lane == i2).astype(F32)
    hot = hot1 + hot2
    r = lax.broadcasted_iota(jnp.int32, (tm, tm), 0)
    c = lax.broadcasted_iota(jnp.int32, (tm, tm), 1)
    before = (c < r).astype(BF16)
    rank = _dot(before, hot.astype(BF16)) + run_ref[...]
    rank1 = jnp.sum(rank * hot1, axis=-1, keepdims=True)
    rank2 = jnp.sum(rank * hot2, axis=-1, keepdims=True)
    vals = (i1.astype(F32), i2.astype(F32), p1 / total, p2 / total, rank1, rank2)
    info = jnp.zeros((tm, LANES), F32)
    for idx, val in enumerate(vals):
        info = jnp.where(lane == idx, val, info)
    info_ref[...] = info
    run_ref[...] += jnp.sum(hot, axis=0, keepdims=True)
    cnt_ref[...] = run_ref[...]


def _router(h, w_pad, b_pad, experts, tm=512):
    s, d = h.shape
    tm = min(tm, s)
    return pl.pallas_call(
        functools.partial(_router_body, experts=experts),
        grid=(s // tm,),
        in_specs=[pl.BlockSpec((tm, d), lambda i: (i, 0)),
                  pl.BlockSpec((d, LANES), lambda i: (0, 0)),
                  pl.BlockSpec((1, LANES), lambda i: (0, 0))],
        out_specs=[pl.BlockSpec((tm, LANES), lambda i: (i, 0)),
                   pl.BlockSpec((1, LANES), lambda i: (0, 0))],
        out_shape=[jax.ShapeDtypeStruct((s, LANES), F32), jax.ShapeDtypeStruct((1, LANES), F32)],
        scratch_shapes=[pltpu.VMEM((1, LANES), F32)],
        compiler_params=_params("arbitrary"),
    )(h, w_pad, b_pad)


def _row_copy(src_ref, src_row, dst_ref, dst_row, sem):
    return pltpu.make_async_copy(src_ref.at[pl.ds(src_row, 1), :], dst_ref.at[pl.ds(dst_row, 1), :], sem)


def _dispatch_body(src_ref, h_ref, o_ref, sem):
    rows = o_ref.shape[0]
    base = pl.program_id(0) * rows

    def copy(n):
        return _row_copy(h_ref, src_ref[base + n], o_ref, n, sem)

    def start(n, carry):
        copy(n).start()
        return carry

    def wait(n, carry):
        copy(n).wait()
        return carry

    lax.fori_loop(0, rows, start, 0, unroll=8)
    lax.fori_loop(0, rows, wait, 0, unroll=8)


def _dispatch(src, h_packed, rows=512):
    d = h_packed.shape[1]
    padded_rows = src.shape[0]
    return pl.pallas_call(
        _dispatch_body,
        grid_spec=pltpu.PrefetchScalarGridSpec(
            num_scalar_prefetch=1,
            grid=(padded_rows // rows,),
            in_specs=[pl.BlockSpec(memory_space=pl.ANY)],
            out_specs=pl.BlockSpec((rows, d), lambda i, src: (i, 0)),
            scratch_shapes=[pltpu.SemaphoreType.DMA(())]),
        out_shape=jax.ShapeDtypeStruct((padded_rows, d), h_packed.dtype),
        compiler_params=_params("arbitrary"),
        name="moe_dispatch",
    )(src, h_packed)


def _unpack_rows(words):
    lo = lax.bitcast_convert_type(words << 16, F32)
    hi = lax.bitcast_convert_type(words & jnp.uint32(0xFFFF0000), F32)
    return jnp.concatenate([lo, hi], axis=1).astype(BF16)


MOE_STEP_SPARE, MOE_STEP_REUSE, MOE_STEP_NEW_WEIGHTS = 0, 1, 2


def _moe_up_body(tile_ref, col_ref, exp_ref, wcol_ref, kind_ref, a_ref, wg_ref, wu_ref, o_ref, wg16_ref, wu16_ref):
    del tile_ref, col_ref, exp_ref, wcol_ref
    kind = kind_ref[pl.program_id(0)]

    @pl.when(kind == MOE_STEP_NEW_WEIGHTS)
    def _():
        wg16_ref[...] = wg_ref[...].astype(BF16)
        wu16_ref[...] = wu_ref[...].astype(BF16)

    @pl.when(kind != MOE_STEP_SPARE)
    def _():
        a = _unpack_rows(a_ref[...])
        g = _dot(a, wg16_ref[...])
        u = _dot(a, wu16_ref[...])
        o_ref[...] = (_silu(g) * u).astype(o_ref.dtype)

    @pl.when(kind == MOE_STEP_SPARE)
    def _():
        o_ref[...] = jnp.zeros_like(o_ref)


def _moe_down_body(tile_ref, col_ref, exp_ref, wcol_ref, kind_ref, a_ref, w_ref, o_ref, w16_ref):
    del tile_ref, col_ref, exp_ref, wcol_ref
    kind = kind_ref[pl.program_id(0)]

    @pl.when(kind == MOE_STEP_NEW_WEIGHTS)
    def _():
        w16_ref[...] = w_ref[...].astype(BF16)

    @pl.when(kind != MOE_STEP_SPARE)
    def _():
        o_ref[...] = _dot(a_ref[...], w16_ref[...])

    @pl.when(kind == MOE_STEP_SPARE)
    def _():
        o_ref[...] = jnp.zeros_like(o_ref)


def _moe_matmul(body, name, sched, a, weights, out_dtype, tm, tn):
    rows = a.shape[0]
    _, k, n = weights[0].shape
    wspec = pl.BlockSpec((None, k, tn), lambda s, tile, col, exp, wcol, valid: (exp[s], 0, wcol[s]))
    return pl.pallas_call(
        body,
        grid_spec=pltpu.PrefetchScalarGridSpec(
            num_scalar_prefetch=5,
            grid=(sched[0].shape[0],),
            in_specs=[pl.BlockSpec((tm, a.shape[1]), lambda s, tile, col, exp, wcol, valid: (tile[s], 0))]
            + [wspec] * len(weights),
            out_specs=pl.BlockSpec((tm, tn), lambda s, tile, col, exp, wcol, valid: (tile[s], col[s])),
            scratch_shapes=[pltpu.VMEM((k, tn), BF16)] * len(weights)),
        out_shape=jax.ShapeDtypeStruct((rows, n), out_dtype),
        compiler_params=_params("arbitrary"),
        name=name,
    )(*sched, a, *weights)


def _combine_body(slot_ref, y_ref, x_ref, info_ref, gate_ref, o_ref, buf_ref, sem, *, tokens):
    base = pl.program_id(0) * tokens

    def copy(n):
        return _row_copy(y_ref, slot_ref[base * MOE_TOP_K + n], buf_ref.at[n % MOE_TOP_K], n // MOE_TOP_K, sem)

    def start(n, carry):
        copy(n).start()
        return carry

    def wait(n, carry):
        copy(n).wait()
        return carry

    lax.fori_loop(0, tokens * MOE_TOP_K, start, 0, unroll=8)
    lax.fori_loop(0, tokens * MOE_TOP_K, wait, 0, unroll=8)
    info = info_ref[...]
    mix = info[:, 2:3] * buf_ref[0] + info[:, 3:4] * buf_ref[1]
    o_ref[...] = x_ref[...] + gate_ref[...] * mix


def _combine(slots, y, x, info, gate, tokens=256):
    s, d = x.shape
    tokens = min(tokens, s)
    return pl.pallas_call(
        functools.partial(_combine_body, tokens=tokens),
        grid_spec=pltpu.PrefetchScalarGridSpec(
            num_scalar_prefetch=1,
            grid=(s // tokens,),
            in_specs=[pl.BlockSpec(memory_space=pl.ANY),
                      pl.BlockSpec((tokens, d), lambda i, slot: (i, 0)),
                      pl.BlockSpec((tokens, LANES), lambda i, slot: (i, 0)),
                      pl.BlockSpec((1, d), lambda i, slot: (0, 0))],
            out_specs=pl.BlockSpec((tokens, d), lambda i, slot: (i, 0)),
            scratch_shapes=[pltpu.VMEM((MOE_TOP_K, tokens, d), F32), pltpu.SemaphoreType.DMA(())]),
        out_shape=jax.ShapeDtypeStruct((s, d), F32),
        compiler_params=_params("arbitrary"),
    )(slots, y, x, info, gate.reshape(1, d))


def _moe_schedule(info, counts, experts, tm, n_col):
    s = info.shape[0]
    max_tiles = (s * MOE_TOP_K) // tm + experts
    ids = info[:, 0:MOE_TOP_K].astype(jnp.int32)
    ranks = info[:, 4:4 + MOE_TOP_K].astype(jnp.int32)
    cnt = counts[0, :experts].astype(jnp.int32)
    tiles = (cnt + tm - 1) // tm
    tile_end = jnp.cumsum(tiles)
    tile_start = tile_end - tiles
    slots = ((tile_start * tm)[ids] + ranks).reshape(-1)
    token = jnp.arange(s * MOE_TOP_K, dtype=jnp.int32) // MOE_TOP_K
    src = jnp.zeros((max_tiles * tm,), jnp.int32).at[slots].set(token)
    step_end = tile_end * n_col
    total = step_end[-1]
    steps = jnp.arange(max_tiles * n_col, dtype=jnp.int32)
    valid = steps < total
    st = jnp.minimum(steps, total - 1)
    exp = jnp.sum(st[:, None] >= step_end[None, :], axis=1)
    local = st - (step_end - tiles * n_col)[exp]
    wcol = local // tiles[exp]
    spare = steps - total
    col = jnp.where(valid, wcol, spare % n_col)
    tile = jnp.where(valid, tile_start[exp] + local % tiles[exp], tile_end[-1] + spare // n_col)
    first = (local % tiles[exp]) == 0
    kind = jnp.where(valid, jnp.where(first, MOE_STEP_NEW_WEIGHTS, MOE_STEP_REUSE), MOE_STEP_SPARE)
    sched = tuple(v.astype(jnp.int32) for v in (tile, col, exp, wcol, kind))
    return slots.astype(jnp.int32), src, sched


def _rope_tables(positions):
    half = ROPE_DIM // 2
    inv_freq = ROPE_THETA ** (-jnp.arange(0, ROPE_DIM, 2, dtype=F32) / ROPE_DIM)
    ang = positions.astype(F32)[:, None] * inv_freq
    cos, sin = jnp.cos(ang), jnp.sin(ang)
    zero = jnp.zeros_like(cos)
    pad = jnp.zeros((cos.shape[0], LANES - ROPE_DIM), F32)
    return (jnp.concatenate([cos, cos, pad], axis=1),
            jnp.concatenate([-sin, zero, pad], axis=1),
            jnp.concatenate([zero, sin, pad], axis=1))


def _pad_lanes(v):
    return jnp.pad(v, ((0, 0), (0, LANES - v.shape[1])))


def kernel(x, c, positions, ada_w, ada_b, norm_mix, norm_ffn, gdn_w_in, gdn_conv, gdn_a_log, gdn_dt_bias, gdn_out_norm, gdn_w_out, kv_norm, w_dkv, kv_latent_norm, w_ukv, k_nope_norm, k_rope_norm, mla_w_dq, mla_q_latent_norm, mla_w_uq, mla_q_nope_norm, mla_q_rope_norm, mla_w_out, ffn_w_gate, ffn_w_up, ffn_w_down, router_w, router_b, moe_w_gate, moe_w_up, moe_w_down):
    assert x.shape[0] == 1 and ada_w.shape[0] == 2
    _, s, d = x.shape
    heads = d // HEAD_DIM
    width = heads * HEAD_DIM
    x0 = x[0]

    mod = _ada_mod(c, ada_w, ada_b).reshape(2, ADA_CHUNKS, d)
    rope_tabs = _rope_tables(positions[0])

    sh_m, sc_m, g_m, sh_f, sc_f, g_f = (mod[0, n] for n in range(ADA_CHUNKS))
    h = _norm_mod(x0, norm_mix[0], sc_m, sh_m, name="norm_mix0")
    w_in = gdn_w_in[0].astype(BF16)
    proj = _matmul(h, w_in[:, :4 * width], name="gdn_in_proj", out_dtype=BF16)
    b_raw = _matmul(h, _pad_lanes(w_in[:, 4 * width:4 * width + heads]), name="gdn_beta_proj", out_dtype=F32)
    a_raw = _matmul(h, _pad_lanes(w_in[:, 4 * width + heads:]), name="gdn_decay_proj", out_dtype=F32)
    qkv = _gdn_conv(proj, gdn_conv[0], width)
    gcb, bb, gct = _gdn_gate(b_raw, a_raw, _pad_lanes(gdn_a_log[0][None]), _pad_lanes(gdn_dt_bias[0][None]), width)
    o = _gdn_chunk(qkv, proj, gcb, bb, gct[:heads], gdn_out_norm[0], heads)
    x1 = _matmul(o, gdn_w_out[0].astype(BF16), name="gdn_out_proj", out_dtype=F32, res=x0, gate=g_m)

    h = _norm_mod(x1, norm_ffn[0], sc_f, sh_f, name="norm_ffn0")
    hid = _swiglu_up(h, ffn_w_gate[0].astype(BF16), ffn_w_up[0].astype(BF16))
    ffn_dim = hid.shape[1]
    x2 = _matmul_ktiled_res(hid, ffn_w_down[0].astype(BF16), x1, g_f, tk=ffn_dim // 2)

    lat = kv_latent_norm.shape[0]
    zeros = jnp.zeros((d,), F32)
    h_kv = _norm_mod(x2, kv_norm, zeros, zeros, name="norm_kv", modulate=False)
    w_dkv16 = w_dkv.astype(BF16)
    ckv, k_rope = _kv_down(h_kv, w_dkv16[:, :lat], _pad_lanes(w_dkv16[:, lat:]), kv_latent_norm[None],
                           _pad_lanes(k_rope_norm[None]), rope_tabs)
    k_cat, v = _kv_up(ckv, w_ukv.astype(BF16), k_rope, k_nope_norm[None])

    sh_m, sc_m, g_m, sh_f, sc_f, g_f = (mod[1, n] for n in range(ADA_CHUNKS))
    h = _norm_mod(x2, norm_mix[1], sc_m, sh_m, name="norm_mix1")
    cq = _q_down(h, mla_w_dq[0].astype(BF16), mla_q_latent_norm[0])
    q_lora = cq.shape[1]
    w_uq = mla_w_uq[0].astype(BF16).reshape(q_lora, heads, HEAD_DIM + ROPE_DIM)
    w_uq = jnp.pad(w_uq, ((0, 0), (0, 0), (0, 2 * HEAD_DIM - HEAD_DIM - ROPE_DIM))).reshape(q_lora, heads * 2 * HEAD_DIM)
    q_cat = _q_up(cq, w_uq, mla_q_nope_norm[0][None], _pad_lanes(mla_q_rope_norm[0][None]), rope_tabs,
                  (HEAD_DIM + ROPE_DIM) ** -0.5 * math.log2(math.e))
    o = _attention(q_cat, k_cat, v, heads)
    x3 = _matmul(o, mla_w_out[0].astype(BF16), name="mla_out_proj", out_dtype=F32, res=x2, gate=g_m)

    experts = router_w.shape[-1]
    moe_tm, moe_tn = 512, 512
    assert moe_w_gate.shape[-1] == d
    h, h_packed = _norm_mod(x3, norm_ffn[1], sc_f, sh_f, name="norm_ffn1", packed=True)
    info, counts = _router(h, _pad_lanes(router_w[0]), _pad_lanes(router_b[0][None]), experts)
    slots, src, sched = _moe_schedule(info, counts, experts, moe_tm, d // moe_tn)
    sorted_h = _dispatch(src, h_packed, moe_tm)
    hid = _moe_matmul(_moe_up_body, "moe_up", sched, sorted_h, (moe_w_gate[0], moe_w_up[0]), BF16, moe_tm, moe_tn)
    y = _moe_matmul(_moe_down_body, "moe_down", sched, hid, (moe_w_down[0],), F32, moe_tm, moe_tn)
    out = _combine(slots, y, x3, info, g_f)
    return out[None]
```

```python
import functools
import math

import jax
import jax.numpy as jnp
from jax import lax
from jax.experimental import pallas as pl
from jax.experimental.pallas import tpu as pltpu

F32 = jnp.float32
BF16 = jnp.bfloat16

RMS_EPS = 1e-6
L2_EPS = 1e-6
NEG_INF = -1e30
ROPE_THETA = 10000.0

LANES = 128
HEAD_DIM = 128
ROPE_DIM = 64
GDN_CHUNK = 64
GDN_CONV = 4
MOE_TOP_K = 2
ADA_CHUNKS = 6
VMEM_LIMIT_BYTES = 56 * 1024 * 1024


def _params(*semantics):
    return pltpu.CompilerParams(dimension_semantics=semantics, vmem_limit_bytes=VMEM_LIMIT_BYTES)


def _dot(a, b):
    return jnp.dot(a, b, preferred_element_type=F32)


def _dot_nt(a, b):
    return lax.dot_general(a, b, (((1,), (1,)), ((), ())), preferred_element_type=F32)


def _dot_tn(a, b):
    return lax.dot_general(a, b, (((0,), (0,)), ((), ())), preferred_element_type=F32)


def _silu(x):
    return x * jax.nn.sigmoid(x)


def _split3(x):
    hi = x.astype(BF16)
    rem = x - hi.astype(F32)
    mid = rem.astype(BF16)
    return hi, mid, (rem - mid.astype(F32)).astype(BF16)


def _rms(x, gain, n):
    ms = jnp.sum(x * x, axis=-1, keepdims=True) * (1.0 / n)
    return x * lax.rsqrt(ms + RMS_EPS) * gain


def _ada_body(c_ref, w_ref, b_ref, o_ref):
    cs = _silu(c_ref[...])
    for j in range(o_ref.shape[-1] // LANES):
        sl = slice(j * LANES, (j + 1) * LANES)
        s = jnp.sum(w_ref[0, :, sl] * cs, axis=0, keepdims=True)
        o_ref[0, :, sl] = s + b_ref[0, :, sl]


def _ada_mod(c, ada_w, ada_b, tn=1024):
    depth, d, n = ada_w.shape
    c_b = jnp.broadcast_to(c.reshape(d, 1), (d, LANES))
    return pl.pallas_call(
        _ada_body,
        grid=(depth, n // tn),
        in_specs=[pl.BlockSpec((d, LANES), lambda l, j: (0, 0)),
                  pl.BlockSpec((1, d, tn), lambda l, j: (l, 0, j)),
                  pl.BlockSpec((1, 1, tn), lambda l, j: (l, 0, j))],
        out_specs=pl.BlockSpec((1, 1, tn), lambda l, j: (l, 0, j)),
        out_shape=jax.ShapeDtypeStruct((depth, 1, n), F32),
        compiler_params=_params("parallel", "parallel"),
    )(c_b, ada_w, ada_b.reshape(depth, 1, n))


def _norm_body(x_ref, g_ref, sc_ref, sh_ref, *o_refs, modulate, packed):
    x = x_ref[...]
    d = x.shape[-1]
    y = _rms(x, g_ref[...], d)
    if modulate:
        y = y * (1.0 + sc_ref[...]) + sh_ref[...]
    yb = y.astype(BF16)
    o_refs[0][...] = yb
    if packed:
        bits = lax.bitcast_convert_type(yb.astype(F32), jnp.uint32)
        lo = bits[:, :d // 2] >> 16
        hi = bits[:, d // 2:] & jnp.uint32(0xFFFF0000)
        o_refs[1][...] = hi | lo


def _norm_mod(x, gain, scale, shift, *, name, modulate=True, packed=False, tr=256):
    s, d = x.shape
    tr = min(tr, s)
    vec = pl.BlockSpec((1, d), lambda i: (0, 0))
    out_shape = [jax.ShapeDtypeStruct((s, d), BF16)]
    out_specs = [pl.BlockSpec((tr, d), lambda i: (i, 0))]
    if packed:
        out_shape.append(jax.ShapeDtypeStruct((s, d // 2), jnp.uint32))
        out_specs.append(pl.BlockSpec((tr, d // 2), lambda i: (i, 0)))
    outs = pl.pallas_call(
        functools.partial(_norm_body, modulate=modulate, packed=packed),
        grid=(s // tr,),
        in_specs=[pl.BlockSpec((tr, d), lambda i: (i, 0)), vec, vec, vec],
        out_specs=out_specs,
        out_shape=out_shape,
        compiler_params=_params("parallel"),
        name=name,
    )(x, gain.reshape(1, d), scale.reshape(1, d), shift.reshape(1, d))
    return outs if packed else outs[0]


def _mm_body(a_ref, w_ref, o_ref):
    o_ref[...] = _dot(a_ref[...], w_ref[...].astype(BF16)).astype(o_ref.dtype)


def _mm_res_body(a_ref, w_ref, r_ref, g_ref, o_ref):
    o_ref[...] = r_ref[...] + g_ref[...] * _dot(a_ref[...], w_ref[...].astype(BF16))


def _matmul(a, w, *, name, out_dtype, tm=1024, tn=512, res=None, gate=None):
    m, k = a.shape
    n = w.shape[1]
    tm, tn = min(tm, m), min(tn, n)
    in_specs = [pl.BlockSpec((tm, k), lambda i, j: (i, 0)),
                pl.BlockSpec((k, tn), lambda i, j: (0, j))]
    args = [a, w]
    body = _mm_body
    if res is not None:
        in_specs += [pl.BlockSpec((tm, tn), lambda i, j: (i, j)),
                     pl.BlockSpec((1, tn), lambda i, j: (0, j))]
        args += [res, gate.reshape(1, n)]
        body = _mm_res_body
    return pl.pallas_call(
        body,
        grid=(m // tm, n // tn),
        in_specs=in_specs,
        out_specs=pl.BlockSpec((tm, tn), lambda i, j: (i, j)),
        out_shape=jax.ShapeDtypeStruct((m, n), out_dtype),
        compiler_params=_params("parallel", "parallel"),
        name=name,
    )(*args)


def _mm_ws_body(a_ref, w_ref, o_ref, w16_ref):
    @pl.when(pl.program_id(1) == 0)
    def _():
        w16_ref[...] = w_ref[...].astype(BF16)

    o_ref[...] = _dot(a_ref[...], w16_ref[...]).astype(o_ref.dtype)


def _matmul_ws(a, w3, n_cols, *, name, out_dtype, tm=512, tn=1024):
    m, k = a.shape
    tm = min(tm, m)
    return pl.pallas_call(
        _mm_ws_body,
        grid=(n_cols // tn, m // tm),
        in_specs=[pl.BlockSpec((tm, k), lambda j, i: (i, 0)),
                  pl.BlockSpec((None, k, tn), lambda j, i: (0, 0, j))],
        out_specs=pl.BlockSpec((tm, tn), lambda j, i: (i, j)),
        out_shape=jax.ShapeDtypeStruct((m, n_cols), out_dtype),
        scratch_shapes=[pltpu.VMEM((k, tn), BF16)],
        compiler_params=_params("parallel", "arbitrary"),
        name=name,
    )(a, w3)


def _mm_ktiled_res_body(a_ref, w_ref, r_ref, g_ref, o_ref, acc_ref):
    kk = pl.program_id(2)

    @pl.when(kk == 0)
    def _():
        acc_ref[...] = jnp.zeros_like(acc_ref)

    acc_ref[...] += _dot(a_ref[...], w_ref[...].astype(BF16))

    @pl.when(kk == pl.num_programs(2) - 1)
    def _():
        o_ref[...] = r_ref[...] + g_ref[...] * acc_ref[...]


def _matmul_ktiled_res(a, w, res, gate, *, tm=512, tn=1024, tk):
    m, k = a.shape
    n = w.shape[1]
    tm, tn = min(tm, m), min(tn, n)
    return pl.pallas_call(
        _mm_ktiled_res_body,
        grid=(m // tm, n // tn, k // tk),
        in_specs=[pl.BlockSpec((tm, tk), lambda i, j, kk: (i, kk)),
                  pl.BlockSpec((tk, tn), lambda i, j, kk: (kk, j)),
                  pl.BlockSpec((tm, tn), lambda i, j, kk: (i, j)),
                  pl.BlockSpec((1, tn), lambda i, j, kk: (0, j))],
        out_specs=pl.BlockSpec((tm, tn), lambda i, j, kk: (i, j)),
        out_shape=jax.ShapeDtypeStruct((m, n), F32),
        scratch_shapes=[pltpu.VMEM((tm, tn), F32)],
        compiler_params=_params("parallel", "parallel", "arbitrary"),
    )(a, w, res, gate.reshape(1, n))


def _swiglu_up_body(a_ref, wg_ref, wu_ref, o_ref):
    a = a_ref[...]
    g = _dot(a, wg_ref[...].astype(BF16))
    u = _dot(a, wu_ref[...].astype(BF16))
    o_ref[...] = (_silu(g) * u).astype(o_ref.dtype)


def _swiglu_up(a, wg, wu, *, tm=1024, tn=256):
    m, k = a.shape
    n = wg.shape[1]
    tm = min(tm, m)
    wspec = pl.BlockSpec((k, tn), lambda i, j: (0, j))
    return pl.pallas_call(
        _swiglu_up_body,
        grid=(m // tm, n // tn),
        in_specs=[pl.BlockSpec((tm, k), lambda i, j: (i, 0)), wspec, wspec],
        out_specs=pl.BlockSpec((tm, tn), lambda i, j: (i, j)),
        out_shape=jax.ShapeDtypeStruct((m, n), BF16),
        compiler_params=_params("parallel", "parallel"),
    )(a, wg, wu)


def _gdn_conv_body(prev_ref, x_ref, w_ref, o_ref):
    i = pl.program_id(0)
    which = pl.program_id(1)
    tr = x_ref.shape[0]
    x16 = x_ref[...]
    prev = prev_ref[...]
    rows2 = jnp.concatenate([jnp.where(i > 0, prev, jnp.zeros_like(prev)), x16], axis=0)
    r = lax.broadcasted_iota(jnp.int32, (tr, 2 * tr), 0)
    c = lax.broadcasted_iota(jnp.int32, (tr, 2 * tr), 1)
    taps = [_dot((c == r + tr - d).astype(BF16), rows2) for d in range(GDN_CONV - 1, 0, -1)]
    for h in range(x_ref.shape[1] // HEAD_DIM):
        sl = slice(h * HEAD_DIM, (h + 1) * HEAD_DIM)
        y = taps[0][:, sl] * w_ref[0:1, sl]
        for j in range(1, GDN_CONV - 1):
            y = y + taps[j][:, sl] * w_ref[j:j + 1, sl]
        y = y + x16[:, sl].astype(F32) * w_ref[GDN_CONV - 1:GDN_CONV, sl]
        y = _silu(y)
        ss = jnp.sum(y * y, axis=-1, keepdims=True)
        yn = y * lax.rsqrt(ss + L2_EPS)
        o_ref[:, sl] = jnp.where(which < 2, yn, y).astype(o_ref.dtype)


def _gdn_conv(proj, conv_w, width, tr=128):
    s = proj.shape[0]
    tr = min(tr, s)
    return pl.pallas_call(
        _gdn_conv_body,
        grid=(s // tr, 3),
        in_specs=[pl.BlockSpec((tr, width), lambda i, c: (jnp.maximum(i - 1, 0), c)),
                  pl.BlockSpec((tr, width), lambda i, c: (i, c)),
                  pl.BlockSpec((GDN_CONV, width), lambda i, c: (0, c))],
        out_specs=pl.BlockSpec((tr, width), lambda i, c: (i, c)),
        out_shape=jax.ShapeDtypeStruct((s, 3 * width), BF16),
        compiler_params=_params("parallel", "parallel"),
        name="gdn_conv",
    )(proj, proj, conv_w)


def _gdn_gate_body(b_ref, a_ref, alog_ref, dtb_ref, gcb_ref, bb_ref, gct_ref):
    tr = b_ref.shape[0]
    beta = jax.nn.sigmoid(b_ref[...])
    x = a_ref[...] + dtb_ref[...]
    softplus = jnp.maximum(x, 0.0) + jnp.log1p(jnp.exp(-jnp.abs(x)))
    g = -jnp.exp(alog_ref[...]) * softplus
    r = lax.broadcasted_iota(jnp.int32, (tr, tr), 0)
    c = lax.broadcasted_iota(jnp.int32, (tr, tr), 1)
    shift = int(math.log2(GDN_CHUNK))
    tri = ((r >> shift == c >> shift) & (c <= r)).astype(BF16)
    gc = sum(_dot(tri, part) for part in _split3(g))
    width = gcb_ref.shape[1]
    er = lax.broadcasted_iota(jnp.int32, (LANES, width), 0)
    ec = lax.broadcasted_iota(jnp.int32, (LANES, width), 1)
    expand = (ec >> int(math.log2(HEAD_DIM)) == er).astype(BF16)
    gc_parts = _split3(gc)
    gcb_ref[...] = sum(_dot(part, expand) for part in gc_parts)
    bb_ref[...] = sum(_dot(part, expand) for part in _split3(beta))
    ir = lax.broadcasted_iota(jnp.int32, (LANES, LANES), 0)
    ic = lax.broadcasted_iota(jnp.int32, (LANES, LANES), 1)
    eye = (ir == ic).astype(BF16)
    gct_ref[...] = sum(_dot_nt(eye, part) for part in gc_parts)


def _gdn_gate(b_raw, a_raw, a_log, dt_bias, width, tr=512):
    s = b_raw.shape[0]
    tr = min(tr, s)
    row = pl.BlockSpec((tr, LANES), lambda i: (i, 0))
    vec = pl.BlockSpec((1, LANES), lambda i: (0, 0))
    wide = pl.BlockSpec((tr, width), lambda i: (i, 0))
    return pl.pallas_call(
        _gdn_gate_body,
        grid=(s // tr,),
        in_specs=[row, row, vec, vec],
        out_specs=[wide, wide, pl.BlockSpec((LANES, tr), lambda i: (0, i))],
        out_shape=[jax.ShapeDtypeStruct((s, width), F32), jax.ShapeDtypeStruct((s, width), F32),
                   jax.ShapeDtypeStruct((LANES, s), F32)],
        compiler_params=_params("parallel"),
    )(b_raw, a_raw, a_log, dt_bias)


def _gdn_chunk_body(q_ref, k_ref, v_ref, gcb_ref, bb_ref, gct_ref, z_ref, gain_ref, o_ref, state_ref):
    @pl.when(pl.program_id(1) == 0)
    def _():
        state_ref[...] = jnp.zeros_like(state_ref)

    cs = GDN_CHUNK
    n_heads = state_ref.shape[0]
    n_chunks = q_ref.shape[0] // cs
    probs = [(j, n) for j in range(n_heads) for n in range(n_chunks)]

    def tile(ref, j, n):
        return ref[n * cs:(n + 1) * cs, j * HEAD_DIM:(j + 1) * HEAD_DIM]

    r = lax.broadcasted_iota(jnp.int32, (cs, cs), 0)
    c = lax.broadcasted_iota(jnp.int32, (cs, cs), 1)
    causal = c <= r
    strict = c < r
    eye = (r == c).astype(F32)

    k = [tile(k_ref, j, n).astype(F32) for j, n in probs]
    k16 = [x.astype(BF16) for x in k]
    gc = [tile(gcb_ref, j, n) for j, n in probs]
    beta = [tile(bb_ref, j, n) for j, n in probs]
    kb = [x * b for x, b in zip(k, beta)]
    decay = []
    for (j, n), g in zip(probs, gc):
        gdiff = g[:, :cs] - gct_ref[j, :, n * cs:(n + 1) * cs]
        decay.append(jnp.where(causal, jnp.exp(jnp.where(causal, gdiff, 0.0)), 0.0))
    q = [tile(q_ref, j, n).astype(F32) * (HEAD_DIM ** -0.5) for j, n in probs]
    kq = [_dot_nt(jnp.concatenate([x.astype(BF16), y.astype(BF16)], axis=0), z) for x, y, z in zip(kb, q, k16)]
    p = [jnp.where(strict, -(x[:cs] * d), 0.0) for x, d in zip(kq, decay)]
    inv = [eye + x for x in p]
    for _ in range(int(math.log2(cs)) - 1):
        p16 = [x.astype(BF16) for x in p]
        p = [_dot(x, x) for x in p16]
        inv = [x + _dot(x.astype(BF16), y.astype(BF16)) for x, y in zip(inv, p)]
    t16 = [x.astype(BF16) for x in inv]
    uw = [_dot(t, jnp.concatenate([(tile(v_ref, j, n).astype(F32) * b).astype(BF16),
                                   (x * jnp.exp(g)).astype(BF16)], axis=1))
          for t, (j, n), b, x, g in zip(t16, probs, beta, kb, gc)]
    qk = [(x[cs:] * d).astype(BF16) for x, d in zip(kq, decay)]
    wq = [jnp.concatenate([x[:, HEAD_DIM:].astype(BF16), (y * jnp.exp(g)).astype(BF16)], axis=0)
          for x, y, g in zip(uw, q, gc)]
    g_last = [g[cs - 1:cs, :] for g in gc]
    k_end = [(x * jnp.exp(gl - g)).astype(BF16) for x, gl, g in zip(k, g_last, gc)]

    state = [state_ref[j] for j in range(n_heads)]
    for n in range(n_chunks):
        idx = [j * n_chunks + n for j in range(n_heads)]
        ws = [_dot(wq[i], s.astype(BF16)) for i, s in zip(idx, state)]
        v16 = [(uw[i][:, :HEAD_DIM] - x[:cs]).astype(BF16) for i, x in zip(idx, ws)]
        o = [x[cs:] + _dot(qk[i], v) for i, x, v in zip(idx, ws, v16)]
        state = [s * jnp.exp(g_last[i]) + _dot_tn(k_end[i], v) for i, s, v in zip(idx, state, v16)]
        for j in range(n_heads):
            z = tile(z_ref, j, n).astype(F32)
            o_ref[n * cs:(n + 1) * cs, j * HEAD_DIM:(j + 1) * HEAD_DIM] = (
                _rms(o[j], gain_ref[...], HEAD_DIM) * _silu(z)).astype(o_ref.dtype)
    for j in range(n_heads):
        state_ref[j] = state[j]


def _gdn_chunk(qkv, proj, gcb, bb, gct, out_norm, heads, rows=256, group=8):
    s = qkv.shape[0]
    rows = min(rows, s)
    groups = heads // group
    blk = lambda off: pl.BlockSpec((rows, group * HEAD_DIM), lambda h, r: (r, off + h))
    return pl.pallas_call(
        _gdn_chunk_body,
        grid=(groups, s // rows),
        in_specs=[blk(0), blk(groups), blk(2 * groups), blk(0), blk(0),
                  pl.BlockSpec((group, 1, rows), lambda h, r: (h, 0, r)),
                  blk(3 * groups),
                  pl.BlockSpec((1, HEAD_DIM), lambda h, r: (0, 0))],
        out_specs=blk(0),
        out_shape=jax.ShapeDtypeStruct((s, heads * HEAD_DIM), BF16),
        scratch_shapes=[pltpu.VMEM((group, HEAD_DIM, HEAD_DIM), F32)],
        compiler_params=_params("parallel", "arbitrary"),
        name="gdn_chunk",
    )(qkv, qkv, qkv, gcb, bb, gct.reshape(gct.shape[0], 1, s), proj, out_norm.reshape(1, HEAD_DIM))


def _rope(x, cos_ref, sin_a_ref, sin_b_ref):
    half = ROPE_DIM // 2
    return (x * cos_ref[...] + pltpu.roll(x, LANES - half, 1) * sin_a_ref[...]
            + pltpu.roll(x, half, 1) * sin_b_ref[...])


def _kv_down_body(a_ref, w1_ref, w2_ref, g1_ref, g2_ref, cos_ref, sa_ref, sb_ref, ckv_ref, kr_ref):
    a = a_ref[...]
    ckv = _dot(a, w1_ref[...].astype(BF16))
    ckv_ref[...] = _rms(ckv, g1_ref[...], ckv.shape[-1]).astype(ckv_ref.dtype)
    kr = _rope(_rms(_dot(a, w2_ref[...].astype(BF16)), g2_ref[...], ROPE_DIM), cos_ref, sa_ref, sb_ref)
    kr_ref[...] = (kr + pltpu.roll(kr, ROPE_DIM, 1)).astype(kr_ref.dtype)


def _kv_down(a, w_c, w_r, g_c, g_r, rope_tabs, tm=512):
    m, k = a.shape
    lat = w_c.shape[1]
    tm = min(tm, m)
    full = lambda shp: pl.BlockSpec(shp, lambda i: (0, 0))
    row = lambda w: pl.BlockSpec((tm, w), lambda i: (i, 0))
    return pl.pallas_call(
        _kv_down_body,
        grid=(m // tm,),
        in_specs=[row(k), full((k, lat)), full((k, LANES)), full((1, lat)), full((1, LANES)),
                  row(LANES), row(LANES), row(LANES)],
        out_specs=[row(lat), row(LANES)],
        out_shape=[jax.ShapeDtypeStruct((m, lat), BF16), jax.ShapeDtypeStruct((m, LANES), BF16)],
        compiler_params=_params("parallel"),
    )(a, w_c, w_r, g_c, g_r, *rope_tabs)


def _kv_up_body(a_ref, w_ref, kr_ref, g_ref, k_ref, v_ref):
    acc = _dot(a_ref[...], w_ref[...].astype(BF16))
    for h in range(acc.shape[1] // (2 * HEAD_DIM)):
        base = 2 * HEAD_DIM * h
        kn = acc[:, base:base + HEAD_DIM]
        k_ref[:, base:base + HEAD_DIM] = _rms(kn, g_ref[...], HEAD_DIM).astype(k_ref.dtype)
        k_ref[:, base + HEAD_DIM:base + 2 * HEAD_DIM] = kr_ref[...]
        v_ref[:, h * HEAD_DIM:(h + 1) * HEAD_DIM] = acc[:, base + HEAD_DIM:base + 2 * HEAD_DIM].astype(v_ref.dtype)


def _kv_up(ckv, w_ukv, kr, g_nope, tm=1024, hps=4):
    m, lat = ckv.shape
    n = w_ukv.shape[1]
    tm = min(tm, m)
    tn = hps * 2 * HEAD_DIM
    return pl.pallas_call(
        _kv_up_body,
        grid=(m // tm, n // tn),
        in_specs=[pl.BlockSpec((tm, lat), lambda i, j: (i, 0)),
                  pl.BlockSpec((lat, tn), lambda i, j: (0, j)),
                  pl.BlockSpec((tm, LANES), lambda i, j: (i, 0)),
                  pl.BlockSpec((1, HEAD_DIM), lambda i, j: (0, 0))],
        out_specs=[pl.BlockSpec((tm, tn), lambda i, j: (i, j)),
                   pl.BlockSpec((tm, tn // 2), lambda i, j: (i, j))],
        out_shape=[jax.ShapeDtypeStruct((m, n), BF16), jax.ShapeDtypeStruct((m, n // 2), BF16)],
        compiler_params=_params("parallel", "parallel"),
    )(ckv, w_ukv, kr, g_nope)


def _q_down_body(a_ref, w_ref, g_ref, o_ref):
    cq = _dot(a_ref[...], w_ref[...].astype(BF16))
    o_ref[...] = _rms(cq, g_ref[...], cq.shape[-1]).astype(o_ref.dtype)


def _q_down(a, w, gain, tm=512):
    m, k = a.shape
    n = w.shape[1]
    tm = min(tm, m)
    return pl.pallas_call(
        _q_down_body,
        grid=(m // tm,),
        in_specs=[pl.BlockSpec((tm, k), lambda i: (i, 0)),
                  pl.BlockSpec((k, n), lambda i: (0, 0)),
                  pl.BlockSpec((1, n), lambda i: (0, 0))],
        out_specs=pl.BlockSpec((tm, n), lambda i: (i, 0)),
        out_shape=jax.ShapeDtypeStruct((m, n), BF16),
        compiler_params=_params("parallel"),
    )(a, w, gain.reshape(1, n))


def _q_up_body(a_ref, w_ref, gn_ref, gr_ref, tab_ref, o_ref, *, scale):
    acc = _dot(a_ref[...], w_ref[...].astype(BF16))
    first = (lax.broadcasted_iota(jnp.int32, (1, LANES), 1) < ROPE_DIM).astype(F32)
    rope_gain = gr_ref[...] * tab_ref[...] * scale
    for h in range(acc.shape[1] // (2 * HEAD_DIM)):
        base = 2 * HEAD_DIM * h
        qn = _rms(acc[:, base:base + HEAD_DIM], gn_ref[...], HEAD_DIM)
        o_ref[:, base:base + HEAD_DIM] = (qn * scale).astype(o_ref.dtype)
        x = acc[:, base + HEAD_DIM:base + 2 * HEAD_DIM]
        ms = jnp.sum(x * x * first, axis=-1, keepdims=True) * (1.0 / ROPE_DIM)
        o_ref[:, base + HEAD_DIM:base + 2 * HEAD_DIM] = (x * lax.rsqrt(ms + RMS_EPS) * rope_gain).astype(o_ref.dtype)


def _q_up(cq, w, g_nope, g_rope, rope_tab, scale, tm=1024, hps=4):
    m, lat = cq.shape
    n = w.shape[1]
    tm = min(tm, m)
    tn = hps * 2 * HEAD_DIM
    vec = pl.BlockSpec((1, LANES), lambda i, j: (0, 0))
    return pl.pallas_call(
        functools.partial(_q_up_body, scale=scale),
        grid=(m // tm, n // tn),
        in_specs=[pl.BlockSpec((tm, lat), lambda i, j: (i, 0)),
                  pl.BlockSpec((lat, tn), lambda i, j: (0, j)),
                  vec, vec, pl.BlockSpec((tm, LANES), lambda i, j: (i, 0))],
        out_specs=pl.BlockSpec((tm, tn), lambda i, j: (i, j)),
        out_shape=jax.ShapeDtypeStruct((m, n), BF16),
        compiler_params=_params("parallel", "parallel"),
        name="mla_q_up",
    )(cq, w, g_nope, g_rope, rope_tab)


def _attn_body(q_ref, k_ref, v_ref, o_ref, s_ref):
    qi = pl.program_id(1)
    t = q_ref.shape[0]
    q = q_ref[...]

    def scores(j, slot):
        s_ref[slot] = _dot_nt(q, k_ref[pl.ds(pl.multiple_of(j * t, t), t), :])

    def update(j, slot, carry, masked):
        m, l, acc = carry
        s = s_ref[slot]
        if masked:
            row = lax.broadcasted_iota(jnp.int32, (t, t), 0)
            col = lax.broadcasted_iota(jnp.int32, (t, t), 1)
            s = jnp.where(col <= row, s, NEG_INF)
        m_new = jnp.maximum(m, jnp.max(s, axis=-1, keepdims=True))
        alpha = jnp.exp2(m - m_new)
        p = jnp.exp2(s - m_new)
        l = alpha * l + jnp.sum(p, axis=-1, keepdims=True)
        v = v_ref[pl.ds(pl.multiple_of(j * t, t), t), :]
        acc = alpha * acc + _dot(p.astype(v.dtype), v)
        return m_new, l, acc

    def pair(i, carry):
        scores(2 * i + 1, 1)
        carry = update(2 * i, 0, carry, False)
        scores(2 * i + 2, 0)
        return update(2 * i + 1, 1, carry, False)

    def odd_tail(carry):
        scores(qi, 1)
        return update(qi, 1, update(qi - 1, 0, carry, False), True)

    def even_tail(carry):
        return update(qi, 0, carry, True)

    scores(0, 0)
    init = (jnp.full((t, 1), NEG_INF, F32), jnp.zeros((t, 1), F32), jnp.zeros((t, v_ref.shape[1]), F32))
    carry = lax.fori_loop(0, qi // 2, pair, init)
    _, l, acc = lax.cond(qi % 2 == 1, odd_tail, even_tail, carry)
    o_ref[...] = (acc / l).astype(o_ref.dtype)


def _attention(q_cat, k_cat, v, heads, t=1024):
    s = q_cat.shape[0]
    t = min(t, s)
    return pl.pallas_call(
        _attn_body,
        grid=(heads, s // t),
        in_specs=[pl.BlockSpec((t, 2 * HEAD_DIM), lambda h, i: (i, h)),
                  pl.BlockSpec((s, 2 * HEAD_DIM), lambda h, i: (0, h)),
                  pl.BlockSpec((s, HEAD_DIM), lambda h, i: (0, h))],
        out_specs=pl.BlockSpec((t, HEAD_DIM), lambda h, i: (i, h)),
        out_shape=jax.ShapeDtypeStruct((s, heads * HEAD_DIM), BF16),
        scratch_shapes=[pltpu.VMEM((2, t, t), F32)],
        compiler_params=_params("parallel", "parallel"),
        name="mla_attention",
    )(q_cat, k_cat, v)


def _router_body(h_ref, w_ref, b_ref, info_ref, cnt_ref, run_ref, *, experts):
    i = pl.program_id(0)

    @pl.when(i == 0)
    def _():
        run_ref[...] = jnp.zeros_like(run_ref)

    h = h_ref[...]
    tm = h.shape[0]
    w = w_ref[...]
    w_hi = w.astype(BF16)
    rem = w - w_hi.astype(F32)
    w_mid = rem.astype(BF16)
    w_lo = (rem - w_mid.astype(F32)).astype(BF16)
    logits = _dot(h, w_hi) + _dot(h, w_mid) + _dot(h, w_lo) + b_ref[...]
    lane = lax.broadcasted_iota(jnp.int32, (tm, LANES), 1)
    logits = jnp.where(lane < experts, logits, NEG_INF)
    e = jnp.exp(logits - jnp.max(logits, axis=-1, keepdims=True))
    probs = e / jnp.sum(e, axis=-1, keepdims=True)
    probs = jnp.where(lane < experts, probs, -1.0)
    p1 = jnp.max(probs, axis=-1, keepdims=True)
    i1 = jnp.min(jnp.where(probs == p1, lane, LANES), axis=-1, keepdims=True)
    rest = jnp.where(lane == i1, -1.0, probs)
    p2 = jnp.max(rest, axis=-1, keepdims=True)
    i2 = jnp.min(jnp.where(rest == p2, lane, LANES), axis=-1, keepdims=True)
    total = p1 + p2
    hot1 = (lane == i1).astype(F32)
    hot2 = (lane == i2).astype(F32)
    hot = hot1 + hot2
    r = lax.broadcasted_iota(jnp.int32, (tm, tm), 0)
    c = lax.broadcasted_iota(jnp.int32, (tm, tm), 1)
    before = (c < r).astype(BF16)
    rank = _dot(before, hot.astype(BF16)) + run_ref[...]
    rank1 = jnp.sum(rank * hot1, axis=-1, keepdims=True)
    rank2 = jnp.sum(rank * hot2, axis=-1, keepdims=True)
    vals = (i1.astype(F32), i2.astype(F32), p1 / total, p2 / total, rank1, rank2)
    info = jnp.zeros((tm, LANES), F32)
    for idx, val in enumerate(vals):
        info = jnp.where(lane == idx, val, info)
    info_ref[...] = info
    run_ref[...] += jnp.sum(hot, axis=0, keepdims=True)
    cnt_ref[...] = run_ref[...]


def _router(h, w_pad, b_pad, experts, tm=512):
    s, d = h.shape
    tm = min(tm, s)
    return pl.pallas_call(
        functools.partial(_router_body, experts=experts),
        grid=(s // tm,),
        in_specs=[pl.BlockSpec((tm, d), lambda i: (i, 0)),
                  pl.BlockSpec((d, LANES), lambda i: (0, 0)),
                  pl.BlockSpec((1, LANES), lambda i: (0, 0))],
        out_specs=[pl.BlockSpec((tm, LANES), lambda i: (i, 0)),
                   pl.BlockSpec((1, LANES), lambda i: (0, 0))],
        out_shape=[jax.ShapeDtypeStruct((s, LANES), F32), jax.ShapeDtypeStruct((1, LANES), F32)],
        scratch_shapes=[pltpu.VMEM((1, LANES), F32)],
        compiler_params=_params("arbitrary"),
    )(h, w_pad, b_pad)


def _row_copy(src_ref, src_row, dst_ref, dst_row, sem):
    return pltpu.make_async_copy(src_ref.at[pl.ds(src_row, 1), :], dst_ref.at[pl.ds(dst_row, 1), :], sem)


def _dispatch_body(src_ref, h_ref, o_ref, sem):
    rows = o_ref.shape[0]
    base = pl.program_id(0) * rows

    def copy(n):
        return _row_copy(h_ref, src_ref[base + n], o_ref, n, sem)

    def start(n, carry):
        copy(n).start()
        return carry

    def wait(n, carry):
        copy(n).wait()
        return carry

    lax.fori_loop(0, rows, start, 0, unroll=8)
    lax.fori_loop(0, rows, wait, 0, unroll=8)


def _dispatch(src, h_packed, rows=512):
    d = h_packed.shape[1]
    padded_rows = src.shape[0]
    return pl.pallas_call(
        _dispatch_body,
        grid_spec=pltpu.PrefetchScalarGridSpec(
            num_scalar_prefetch=1,
            grid=(padded_rows // rows,),
            in_specs=[pl.BlockSpec(memory_space=pl.ANY)],
            out_specs=pl.BlockSpec((rows, d), lambda i, src: (i, 0)),
            scratch_shapes=[pltpu.SemaphoreType.DMA(())]),
        out_shape=jax.ShapeDtypeStruct((padded_rows, d), h_packed.dtype),
        compiler_params=_params("arbitrary"),
        name="moe_dispatch",
    )(src, h_packed)


def _unpack_rows(words):
    lo = lax.bitcast_convert_type(words << 16, F32)
    hi = lax.bitcast_convert_type(words & jnp.uint32(0xFFFF0000), F32)
    return jnp.concatenate([lo, hi], axis=1).astype(BF16)


MOE_STEP_SPARE, MOE_STEP_REUSE, MOE_STEP_NEW_WEIGHTS = 0, 1, 2


def _moe_up_body(tile_ref, col_ref, exp_ref, wcol_ref, kind_ref, a_ref, wg_ref, wu_ref, o_ref, wg16_ref, wu16_ref):
    del tile_ref, col_ref, exp_ref, wcol_ref
    kind = kind_ref[pl.program_id(0)]

    @pl.when(kind == MOE_STEP_NEW_WEIGHTS)
    def _():
        wg16_ref[...] = wg_ref[...].astype(BF16)
        wu16_ref[...] = wu_ref[...].astype(BF16)

    @pl.when(kind != MOE_STEP_SPARE)
    def _():
        a = _unpack_rows(a_ref[...])
        g = _dot(a, wg16_ref[...])
        u = _dot(a, wu16_ref[...])
        o_ref[...] = (_silu(g) * u).astype(o_ref.dtype)

    @pl.when(kind == MOE_STEP_SPARE)
    def _():
        o_ref[...] = jnp.zeros_like(o_ref)


def _moe_down_body(tile_ref, col_ref, exp_ref, wcol_ref, kind_ref, a_ref, w_ref, o_ref, w16_ref):
    del tile_ref, col_ref, exp_ref, wcol_ref
    kind = kind_ref[pl.program_id(0)]

    @pl.when(kind == MOE_STEP_NEW_WEIGHTS)
    def _():
        w16_ref[...] = w_ref[...].astype(BF16)

    @pl.when(kind != MOE_STEP_SPARE)
    def _():
        o_ref[...] = _dot(a_ref[...], w16_ref[...])

    @pl.when(kind == MOE_STEP_SPARE)
    def _():
        o_ref[...] = jnp.zeros_like(o_ref)


def _moe_matmul(body, name, sched, a, weights, out_dtype, tm, tn):
    rows = a.shape[0]
    _, k, n = weights[0].shape
    wspec = pl.BlockSpec((None, k, tn), lambda s, tile, col, exp, wcol, valid: (exp[s], 0, wcol[s]))
    return pl.pallas_call(
        body,
        grid_spec=pltpu.PrefetchScalarGridSpec(
            num_scalar_prefetch=5,
            grid=(sched[0].shape[0],),
            in_specs=[pl.BlockSpec((tm, a.shape[1]), lambda s, tile, col, exp, wcol, valid: (tile[s], 0))]
            + [wspec] * len(weights),
            out_specs=pl.BlockSpec((tm, tn), lambda s, tile, col, exp, wcol, valid: (tile[s], col[s])),
            scratch_shapes=[pltpu.VMEM((k, tn), BF16)] * len(weights)),
        out_shape=jax.ShapeDtypeStruct((rows, n), out_dtype),
        compiler_params=_params("arbitrary"),
        name=name,
    )(*sched, a, *weights)


def _combine_body(slot_ref, y_ref, x_ref, info_ref, gate_ref, o_ref, buf_ref, sem, *, tokens):
    base = pl.program_id(0) * tokens

    def copy(n):
        return _row_copy(y_ref, slot_ref[base * MOE_TOP_K + n], buf_ref.at[n % MOE_TOP_K], n // MOE_TOP_K, sem)

    def start(n, carry):
        copy(n).start()
        return carry

    def wait(n, carry):
        copy(n).wait()
        return carry

    lax.fori_loop(0, tokens * MOE_TOP_K, start, 0, unroll=8)
    lax.fori_loop(0, tokens * MOE_TOP_K, wait, 0, unroll=8)
    info = info_ref[...]
    mix = info[:, 2:3] * buf_ref[0] + info[:, 3:4] * buf_ref[1]
    o_ref[...] = x_ref[...] + gate_ref[...] * mix


def _combine(slots, y, x, info, gate, tokens=256):
    s, d = x.shape
    tokens = min(tokens, s)
    return pl.pallas_call(
        functools.partial(_combine_body, tokens=tokens),
        grid_spec=pltpu.PrefetchScalarGridSpec(
            num_scalar_prefetch=1,
            grid=(s // tokens,),
            in_specs=[pl.BlockSpec(memory_space=pl.ANY),
                      pl.BlockSpec((tokens, d), lambda i, slot: (i, 0)),
                      pl.BlockSpec((tokens, LANES), lambda i, slot: (i, 0)),
                      pl.BlockSpec((1, d), lambda i, slot: (0, 0))],
            out_specs=pl.BlockSpec((tokens, d), lambda i, slot: (i, 0)),
            scratch_shapes=[pltpu.VMEM((MOE_TOP_K, tokens, d), F32), pltpu.SemaphoreType.DMA(())]),
        out_shape=jax.ShapeDtypeStruct((s, d), F32),
        compiler_params=_params("arbitrary"),
    )(slots, y, x, info, gate.reshape(1, d))


def _moe_schedule(info, counts, experts, tm, n_col):
    s = info.shape[0]
    max_tiles = (s * MOE_TOP_K) // tm + experts
    ids = info[:, 0:MOE_TOP_K].astype(jnp.int32)
    ranks = info[:, 4:4 + MOE_TOP_K].astype(jnp.int32)
    cnt = counts[0, :experts].astype(jnp.int32)
    tiles = (cnt + tm - 1) // tm
    tile_end = jnp.cumsum(tiles)
    tile_start = tile_end - tiles
    slots = ((tile_start * tm)[ids] + ranks).reshape(-1)
    token = jnp.arange(s * MOE_TOP_K, dtype=jnp.int32) // MOE_TOP_K
    src = jnp.zeros((max_tiles * tm,), jnp.int32).at[slots].set(token)
    step_end = tile_end * n_col
    total = step_end[-1]
    steps = jnp.arange(max_tiles * n_col, dtype=jnp.int32)
    valid = steps < total
    st = jnp.minimum(steps, total - 1)
    exp = jnp.sum(st[:, None] >= step_end[None, :], axis=1)
    local = st - (step_end - tiles * n_col)[exp]
    wcol = local // tiles[exp]
    spare = steps - total
    col = jnp.where(valid, wcol, spare % n_col)
    tile = jnp.where(valid, tile_start[exp] + local % tiles[exp], tile_end[-1] + spare // n_col)
    first = (local % tiles[exp]) == 0
    kind = jnp.where(valid, jnp.where(first, MOE_STEP_NEW_WEIGHTS, MOE_STEP_REUSE), MOE_STEP_SPARE)
    sched = tuple(v.astype(jnp.int32) for v in (tile, col, exp, wcol, kind))
    return slots.astype(jnp.int32), src, sched


def _rope_tables(positions):
    inv_freq = ROPE_THETA ** (-jnp.arange(0, ROPE_DIM, 2, dtype=F32) / ROPE_DIM)
    ang = positions.astype(F32)[:, None] * inv_freq
    cos, sin = jnp.cos(ang), jnp.sin(ang)
    zero = jnp.zeros_like(cos)
    pad = jnp.zeros((cos.shape[0], LANES - ROPE_DIM), F32)
    key_tabs = (jnp.concatenate([cos, cos, pad], axis=1),
                jnp.concatenate([-sin, zero, pad], axis=1),
                jnp.concatenate([zero, sin, pad], axis=1))
    return key_tabs, jnp.concatenate([cos, cos, -sin, sin], axis=1)


def _pad_lanes(v):
    return jnp.pad(v, ((0, 0), (0, LANES - v.shape[1])))


def kernel(x, c, positions, ada_w, ada_b, norm_mix, norm_ffn, gdn_w_in, gdn_conv, gdn_a_log, gdn_dt_bias, gdn_out_norm, gdn_w_out, kv_norm, w_dkv, kv_latent_norm, w_ukv, k_nope_norm, k_rope_norm, mla_w_dq, mla_q_latent_norm, mla_w_uq, mla_q_nope_norm, mla_q_rope_norm, mla_w_out, ffn_w_gate, ffn_w_up, ffn_w_down, router_w, router_b, moe_w_gate, moe_w_up, moe_w_down):
    assert x.shape[0] == 1 and ada_w.shape[0] == 2
    _, s, d = x.shape
    heads = d // HEAD_DIM
    width = heads * HEAD_DIM
    x0 = x[0]

    mod = _ada_mod(c, ada_w, ada_b).reshape(2, ADA_CHUNKS, d)
    rope_tabs, q_rope_tab = _rope_tables(positions[0])

    sh_m, sc_m, g_m, sh_f, sc_f, g_f = (mod[0, n] for n in range(ADA_CHUNKS))
    h = _norm_mod(x0, norm_mix[0], sc_m, sh_m, name="norm_mix0")
    proj = _matmul_ws(h, gdn_w_in, 4 * width, name="gdn_in_proj", out_dtype=BF16)
    w_ba = gdn_w_in[0, :, 4 * width:].astype(BF16)
    b_raw = _matmul(h, _pad_lanes(w_ba[:, :heads]), name="gdn_beta_proj", out_dtype=F32)
    a_raw = _matmul(h, _pad_lanes(w_ba[:, heads:]), name="gdn_decay_proj", out_dtype=F32)
    qkv = _gdn_conv(proj, gdn_conv[0], width)
    gcb, bb, gct = _gdn_gate(b_raw, a_raw, _pad_lanes(gdn_a_log[0][None]), _pad_lanes(gdn_dt_bias[0][None]), width)
    o = _gdn_chunk(qkv, proj, gcb, bb, gct[:heads], gdn_out_norm[0], heads)
    x1 = _matmul(o, gdn_w_out[0].astype(BF16), name="gdn_out_proj", out_dtype=F32, res=x0, gate=g_m)

    h = _norm_mod(x1, norm_ffn[0], sc_f, sh_f, name="norm_ffn0")
    hid = _swiglu_up(h, ffn_w_gate[0].astype(BF16), ffn_w_up[0].astype(BF16))
    ffn_dim = hid.shape[1]
    x2 = _matmul_ktiled_res(hid, ffn_w_down[0].astype(BF16), x1, g_f, tk=ffn_dim // 2)

    lat = kv_latent_norm.shape[0]
    zeros = jnp.zeros((d,), F32)
    h_kv = _norm_mod(x2, kv_norm, zeros, zeros, name="norm_kv", modulate=False)
    w_dkv16 = w_dkv.astype(BF16)
    ckv, k_rope = _kv_down(h_kv, w_dkv16[:, :lat], _pad_lanes(w_dkv16[:, lat:]), kv_latent_norm[None],
                           _pad_lanes(k_rope_norm[None]), rope_tabs)
    k_cat, v = _kv_up(ckv, w_ukv.astype(BF16), k_rope, k_nope_norm[None])

    sh_m, sc_m, g_m, sh_f, sc_f, g_f = (mod[1, n] for n in range(ADA_CHUNKS))
    h = _norm_mod(x2, norm_mix[1], sc_m, sh_m, name="norm_mix1")
    cq = _q_down(h, mla_w_dq[0].astype(BF16), mla_q_latent_norm[0])
    q_lora = cq.shape[1]
    w_uq = mla_w_uq[0].astype(BF16).reshape(q_lora, heads, HEAD_DIM + ROPE_DIM)
    half = ROPE_DIM // 2
    swap = lambda t: jnp.concatenate([t[..., half:], t[..., :half]], axis=-1)
    w_rope = w_uq[..., HEAD_DIM:]
    w_uq = jnp.concatenate([w_uq, swap(w_rope)], axis=-1).reshape(q_lora, heads * 2 * HEAD_DIM)
    g_rope = mla_q_rope_norm[0][None]
    q_cat = _q_up(cq, w_uq, mla_q_nope_norm[0][None], jnp.concatenate([g_rope, swap(g_rope)], axis=1),
                  q_rope_tab, (HEAD_DIM + ROPE_DIM) ** -0.5 * math.log2(math.e))
    o = _attention(q_cat, k_cat, v, heads)
    x3 = _matmul(o, mla_w_out[0].astype(BF16), name="mla_out_proj", out_dtype=F32, res=x2, gate=g_m)

    experts = router_w.shape[-1]
    moe_tm, moe_tn = 512, 512
    assert moe_w_gate.shape[-1] == d
    h, h_packed = _norm_mod(x3, norm_ffn[1], sc_f, sh_f, name="norm_ffn1", packed=True)
    info, counts = _router(h, _pad_lanes(router_w[0]), _pad_lanes(router_b[0][None]), experts)
    slots, src, sched = _moe_schedule(info, counts, experts, moe_tm, d // moe_tn)
    sorted_h = _dispatch(src, h_packed, moe_tm)
    hid = _moe_matmul(_moe_up_body, "moe_up", sched, sorted_h, (moe_w_gate[0], moe_w_up[0]), BF16, moe_tm, moe_tn)
    y = _moe_matmul(_moe_down_body, "moe_down", sched, hid, (moe_w_down[0],), F32, moe_tm, moe_tn)
    out = _combine(slots, y, x3, info, g_f)
    return out[None]
```

```python
import functools
import math

import jax
import jax.numpy as jnp
from jax import lax
from jax.experimental import pallas as pl
from jax.experimental.pallas import tpu as pltpu

F32 = jnp.float32
BF16 = jnp.bfloat16

RMS_EPS = 1e-6
L2_EPS = 1e-6
NEG_INF = -1e30
ROPE_THETA = 10000.0

LANES = 128
HEAD_DIM = 128
ROPE_DIM = 64
GDN_CHUNK = 64
GDN_CONV = 4
MOE_TOP_K = 2
ADA_CHUNKS = 6
VMEM_LIMIT_BYTES = 56 * 1024 * 1024


def _params(*semantics):
    return pltpu.CompilerParams(dimension_semantics=semantics, vmem_limit_bytes=VMEM_LIMIT_BYTES)


def _dot(a, b):
    return jnp.dot(a, b, preferred_element_type=F32)


def _dot_nt(a, b):
    return lax.dot_general(a, b, (((1,), (1,)), ((), ())), preferred_element_type=F32)


def _dot_tn(a, b):
    return lax.dot_general(a, b, (((0,), (0,)), ((), ())), preferred_element_type=F32)


def _silu(x):
    return x * jax.nn.sigmoid(x)


def _split3(x):
    hi = x.astype(BF16)
    rem = x - hi.astype(F32)
    mid = rem.astype(BF16)
    return hi, mid, (rem - mid.astype(F32)).astype(BF16)


def _rms(x, gain, n):
    ms = jnp.sum(x * x, axis=-1, keepdims=True) * (1.0 / n)
    return x * lax.rsqrt(ms + RMS_EPS) * gain


def _ada_body(c_ref, w_ref, b_ref, o_ref):
    cs = _silu(c_ref[...])
    for j in range(o_ref.shape[-1] // LANES):
        sl = slice(j * LANES, (j + 1) * LANES)
        s = jnp.sum(w_ref[0, :, sl] * cs, axis=0, keepdims=True)
        o_ref[0, :, sl] = s + b_ref[0, :, sl]


def _ada_mod(c, ada_w, ada_b, tn=1024):
    depth, d, n = ada_w.shape
    c_b = jnp.broadcast_to(c.reshape(d, 1), (d, LANES))
    return pl.pallas_call(
        _ada_body,
        grid=(depth, n // tn),
        in_specs=[pl.BlockSpec((d, LANES), lambda l, j: (0, 0)),
                  pl.BlockSpec((1, d, tn), lambda l, j: (l, 0, j)),
                  pl.BlockSpec((1, 1, tn), lambda l, j: (l, 0, j))],
        out_specs=pl.BlockSpec((1, 1, tn), lambda l, j: (l, 0, j)),
        out_shape=jax.ShapeDtypeStruct((depth, 1, n), F32),
        compiler_params=_params("parallel", "parallel"),
    )(c_b, ada_w, ada_b.reshape(depth, 1, n))


def _norm_body(x_ref, g_ref, sc_ref, sh_ref, o_ref, *, modulate):
    x = x_ref[...]
    y = _rms(x, g_ref[...], x.shape[-1])
    if modulate:
        y = y * (1.0 + sc_ref[...]) + sh_ref[...]
    o_ref[...] = y.astype(o_ref.dtype)


def _norm_mod(x, gain, scale, shift, *, name, modulate=True, out_dtype=BF16, tr=256):
    s, d = x.shape
    tr = min(tr, s)
    vec = pl.BlockSpec((1, d), lambda i: (0, 0))
    return pl.pallas_call(
        functools.partial(_norm_body, modulate=modulate),
        grid=(s // tr,),
        in_specs=[pl.BlockSpec((tr, d), lambda i: (i, 0)), vec, vec, vec],
        out_specs=pl.BlockSpec((tr, d), lambda i: (i, 0)),
        out_shape=jax.ShapeDtypeStruct((s, d), out_dtype),
        compiler_params=_params("parallel"),
        name=name,
    )(x, gain.reshape(1, d), scale.reshape(1, d), shift.reshape(1, d))


def _mm_body(a_ref, w_ref, o_ref):
    o_ref[...] = _dot(a_ref[...], w_ref[...].astype(BF16)).astype(o_ref.dtype)


def _mm_res_body(a_ref, w_ref, r_ref, g_ref, o_ref):
    o_ref[...] = r_ref[...] + g_ref[...] * _dot(a_ref[...], w_ref[...].astype(BF16))


def _matmul(a, w, *, name, out_dtype, tm=1024, tn=512, res=None, gate=None):
    m, k = a.shape
    n = w.shape[1]
    tm, tn = min(tm, m), min(tn, n)
    in_specs = [pl.BlockSpec((tm, k), lambda i, j: (i, 0)),
                pl.BlockSpec((k, tn), lambda i, j: (0, j))]
    args = [a, w]
    body = _mm_body
    if res is not None:
        in_specs += [pl.BlockSpec((tm, tn), lambda i, j: (i, j)),
                     pl.BlockSpec((1, tn), lambda i, j: (0, j))]
        args += [res, gate.reshape(1, n)]
        body = _mm_res_body
    return pl.pallas_call(
        body,
        grid=(m // tm, n // tn),
        in_specs=in_specs,
        out_specs=pl.BlockSpec((tm, tn), lambda i, j: (i, j)),
        out_shape=jax.ShapeDtypeStruct((m, n), out_dtype),
        compiler_params=_params("parallel", "parallel"),
        name=name,
    )(*args)


def _mm_ws_body(a_ref, w_ref, o_ref, w16_ref):
    @pl.when(pl.program_id(1) == 0)
    def _():
        w16_ref[...] = w_ref[...].astype(BF16)

    o_ref[...] = _dot(a_ref[...], w16_ref[...]).astype(o_ref.dtype)


def _matmul_ws(a, w, n_cols, *, name, out_dtype, tm=512, tn=1024):
    m, k = a.shape
    tm = min(tm, m)
    return pl.pallas_call(
        _mm_ws_body,
        grid=(n_cols // tn, m // tm),
        in_specs=[pl.BlockSpec((tm, k), lambda j, i: (i, 0)),
                  pl.BlockSpec((k, tn), lambda j, i: (0, j))],
        out_specs=pl.BlockSpec((tm, tn), lambda j, i: (i, j)),
        out_shape=jax.ShapeDtypeStruct((m, n_cols), out_dtype),
        scratch_shapes=[pltpu.VMEM((k, tn), BF16)],
        compiler_params=_params("parallel", "arbitrary"),
        name=name,
    )(a, w)


def _mm_ktiled_res_body(a_ref, w_ref, r_ref, g_ref, o_ref, acc_ref):
    kk = pl.program_id(2)

    @pl.when(kk == 0)
    def _():
        acc_ref[...] = jnp.zeros_like(acc_ref)

    acc_ref[...] += _dot(a_ref[...], w_ref[...].astype(BF16))

    @pl.when(kk == pl.num_programs(2) - 1)
    def _():
        o_ref[...] = r_ref[...] + g_ref[...] * acc_ref[...]


def _matmul_ktiled_res(a, w, res, gate, *, tm=512, tn=1024, tk):
    m, k = a.shape
    n = w.shape[1]
    tm, tn = min(tm, m), min(tn, n)
    return pl.pallas_call(
        _mm_ktiled_res_body,
        grid=(m // tm, n // tn, k // tk),
        in_specs=[pl.BlockSpec((tm, tk), lambda i, j, kk: (i, kk)),
                  pl.BlockSpec((tk, tn), lambda i, j, kk: (kk, j)),
                  pl.BlockSpec((tm, tn), lambda i, j, kk: (i, j)),
                  pl.BlockSpec((1, tn), lambda i, j, kk: (0, j))],
        out_specs=pl.BlockSpec((tm, tn), lambda i, j, kk: (i, j)),
        out_shape=jax.ShapeDtypeStruct((m, n), F32),
        scratch_shapes=[pltpu.VMEM((tm, tn), F32)],
        compiler_params=_params("parallel", "parallel", "arbitrary"),
    )(a, w, res, gate.reshape(1, n))


def _swiglu_up_body(a_ref, wg_ref, wu_ref, o_ref):
    a = a_ref[...]
    g = _dot(a, wg_ref[...].astype(BF16))
    u = _dot(a, wu_ref[...].astype(BF16))
    o_ref[...] = (_silu(g) * u).astype(o_ref.dtype)


def _swiglu_up(a, wg, wu, *, tm=1024, tn=256):
    m, k = a.shape
    n = wg.shape[1]
    tm = min(tm, m)
    wspec = pl.BlockSpec((k, tn), lambda i, j: (0, j))
    return pl.pallas_call(
        _swiglu_up_body,
        grid=(m // tm, n // tn),
        in_specs=[pl.BlockSpec((tm, k), lambda i, j: (i, 0)), wspec, wspec],
        out_specs=pl.BlockSpec((tm, tn), lambda i, j: (i, j)),
        out_shape=jax.ShapeDtypeStruct((m, n), BF16),
        compiler_params=_params("parallel", "parallel"),
    )(a, wg, wu)


def _gdn_conv_body(prev_ref, x_ref, w_ref, o_ref):
    i = pl.program_id(0)
    which = pl.program_id(1)
    tr = x_ref.shape[0]
    x16 = x_ref[...]
    prev = prev_ref[...]
    rows2 = jnp.concatenate([jnp.where(i > 0, prev, jnp.zeros_like(prev)), x16], axis=0)
    r = lax.broadcasted_iota(jnp.int32, (tr, 2 * tr), 0)
    c = lax.broadcasted_iota(jnp.int32, (tr, 2 * tr), 1)
    taps = [_dot((c == r + tr - d).astype(BF16), rows2) for d in range(GDN_CONV - 1, 0, -1)]
    for h in range(x_ref.shape[1] // HEAD_DIM):
        sl = slice(h * HEAD_DIM, (h + 1) * HEAD_DIM)
        y = taps[0][:, sl] * w_ref[0:1, sl]
        for j in range(1, GDN_CONV - 1):
            y = y + taps[j][:, sl] * w_ref[j:j + 1, sl]
        y = y + x16[:, sl].astype(F32) * w_ref[GDN_CONV - 1:GDN_CONV, sl]
        y = _silu(y)
        ss = jnp.sum(y * y, axis=-1, keepdims=True)
        yn = y * lax.rsqrt(ss + L2_EPS)
        o_ref[:, sl] = jnp.where(which < 2, yn, y).astype(o_ref.dtype)


def _gdn_conv(proj, conv_w, width, tr=128):
    s = proj.shape[0]
    tr = min(tr, s)
    return pl.pallas_call(
        _gdn_conv_body,
        grid=(s // tr, 3),
        in_specs=[pl.BlockSpec((tr, width), lambda i, c: (jnp.maximum(i - 1, 0), c)),
                  pl.BlockSpec((tr, width), lambda i, c: (i, c)),
                  pl.BlockSpec((GDN_CONV, width), lambda i, c: (0, c))],
        out_specs=pl.BlockSpec((tr, width), lambda i, c: (i, c)),
        out_shape=jax.ShapeDtypeStruct((s, 3 * width), BF16),
        compiler_params=_params("parallel", "parallel"),
        name="gdn_conv",
    )(proj, proj, conv_w)


def _gdn_gate_body(b_ref, a_ref, alog_ref, dtb_ref, gcb_ref, bb_ref, gct_ref):
    tr = b_ref.shape[0]
    beta = jax.nn.sigmoid(b_ref[...])
    x = a_ref[...] + dtb_ref[...]
    softplus = jnp.maximum(x, 0.0) + jnp.log1p(jnp.exp(-jnp.abs(x)))
    g = -jnp.exp(alog_ref[...]) * softplus
    r = lax.broadcasted_iota(jnp.int32, (tr, tr), 0)
    c = lax.broadcasted_iota(jnp.int32, (tr, tr), 1)
    shift = int(math.log2(GDN_CHUNK))
    tri = ((r >> shift == c >> shift) & (c <= r)).astype(BF16)
    gc = sum(_dot(tri, part) for part in _split3(g))
    width = gcb_ref.shape[1]
    er = lax.broadcasted_iota(jnp.int32, (LANES, width), 0)
    ec = lax.broadcasted_iota(jnp.int32, (LANES, width), 1)
    expand = (ec >> int(math.log2(HEAD_DIM)) == er).astype(BF16)
    gc_parts = _split3(gc)
    gcb_ref[...] = sum(_dot(part, expand) for part in gc_parts)
    bb_ref[...] = sum(_dot(part, expand) for part in _split3(beta))
    ir = lax.broadcasted_iota(jnp.int32, (LANES, LANES), 0)
    ic = lax.broadcasted_iota(jnp.int32, (LANES, LANES), 1)
    eye = (ir == ic).astype(BF16)
    gct_ref[...] = sum(_dot_nt(eye, part) for part in gc_parts)


def _gdn_gate(b_raw, a_raw, a_log, dt_bias, width, tr=512):
    s = b_raw.shape[0]
    tr = min(tr, s)
    row = pl.BlockSpec((tr, LANES), lambda i: (i, 0))
    vec = pl.BlockSpec((1, LANES), lambda i: (0, 0))
    wide = pl.BlockSpec((tr, width), lambda i: (i, 0))
    return pl.pallas_call(
        _gdn_gate_body,
        grid=(s // tr,),
        in_specs=[row, row, vec, vec],
        out_specs=[wide, wide, pl.BlockSpec((LANES, tr), lambda i: (0, i))],
        out_shape=[jax.ShapeDtypeStruct((s, width), F32), jax.ShapeDtypeStruct((s, width), F32),
                   jax.ShapeDtypeStruct((LANES, s), F32)],
        compiler_params=_params("parallel"),
    )(b_raw, a_raw, a_log, dt_bias)


def _gdn_chunk_body(q_ref, k_ref, v_ref, gcb_ref, bb_ref, gct_ref, z_ref, gain_ref, o_ref, state_ref):
    @pl.when(pl.program_id(1) == 0)
    def _():
        state_ref[...] = jnp.zeros_like(state_ref)

    cs = GDN_CHUNK
    n_heads = state_ref.shape[0]
    n_chunks = q_ref.shape[0] // cs
    probs = [(j, n) for j in range(n_heads) for n in range(n_chunks)]

    def tile(ref, j, n):
        return ref[n * cs:(n + 1) * cs, j * HEAD_DIM:(j + 1) * HEAD_DIM]

    r = lax.broadcasted_iota(jnp.int32, (cs, cs), 0)
    c = lax.broadcasted_iota(jnp.int32, (cs, cs), 1)
    causal = c <= r
    strict = c < r
    eye = (r == c).astype(F32)

    k = [tile(k_ref, j, n).astype(F32) for j, n in probs]
    k16 = [x.astype(BF16) for x in k]
    gc = [tile(gcb_ref, j, n) for j, n in probs]
    beta = [tile(bb_ref, j, n) for j, n in probs]
    kb = [x * b for x, b in zip(k, beta)]
    decay = []
    for (j, n), g in zip(probs, gc):
        gdiff = g[:, :cs] - gct_ref[j, :, n * cs:(n + 1) * cs]
        decay.append(jnp.where(causal, jnp.exp(jnp.where(causal, gdiff, 0.0)), 0.0))
    q = [tile(q_ref, j, n).astype(F32) * (HEAD_DIM ** -0.5) for j, n in probs]
    kq = [_dot_nt(jnp.concatenate([x.astype(BF16), y.astype(BF16)], axis=0), z) for x, y, z in zip(kb, q, k16)]
    p = [jnp.where(strict, -(x[:cs] * d), 0.0) for x, d in zip(kq, decay)]
    inv = [eye + x for x in p]
    for _ in range(int(math.log2(cs)) - 1):
        p16 = [x.astype(BF16) for x in p]
        p = [_dot(x, x) for x in p16]
        inv = [x + _dot(x.astype(BF16), y.astype(BF16)) for x, y in zip(inv, p)]
    t16 = [x.astype(BF16) for x in inv]
    uw = [_dot(t, jnp.concatenate([(tile(v_ref, j, n).astype(F32) * b).astype(BF16),
                                   (x * jnp.exp(g)).astype(BF16)], axis=1))
          for t, (j, n), b, x, g in zip(t16, probs, beta, kb, gc)]
    qk = [(x[cs:] * d).astype(BF16) for x, d in zip(kq, decay)]
    wq = [jnp.concatenate([x[:, HEAD_DIM:].astype(BF16), (y * jnp.exp(g)).astype(BF16)], axis=0)
          for x, y, g in zip(uw, q, gc)]
    g_last = [g[cs - 1:cs, :] for g in gc]
    k_end = [(x * jnp.exp(gl - g)).astype(BF16) for x, gl, g in zip(k, g_last, gc)]

    state = [state_ref[j] for j in range(n_heads)]
    for n in range(n_chunks):
        idx = [j * n_chunks + n for j in range(n_heads)]
        ws = [_dot(wq[i], s.astype(BF16)) for i, s in zip(idx, state)]
        v16 = [(uw[i][:, :HEAD_DIM] - x[:cs]).astype(BF16) for i, x in zip(idx, ws)]
        o = [x[cs:] + _dot(qk[i], v) for i, x, v in zip(idx, ws, v16)]
        state = [s * jnp.exp(g_last[i]) + _dot_tn(k_end[i], v) for i, s, v in zip(idx, state, v16)]
        for j in range(n_heads):
            z = tile(z_ref, j, n).astype(F32)
            o_ref[n * cs:(n + 1) * cs, j * HEAD_DIM:(j + 1) * HEAD_DIM] = (
                _rms(o[j], gain_ref[...], HEAD_DIM) * _silu(z)).astype(o_ref.dtype)
    for j in range(n_heads):
        state_ref[j] = state[j]


def _gdn_chunk(qkv, proj, gcb, bb, gct, out_norm, heads, rows=256, group=8):
    s = qkv.shape[0]
    rows = min(rows, s)
    groups = heads // group
    blk = lambda off: pl.BlockSpec((rows, group * HEAD_DIM), lambda h, r: (r, off + h))
    return pl.pallas_call(
        _gdn_chunk_body,
        grid=(groups, s // rows),
        in_specs=[blk(0), blk(groups), blk(2 * groups), blk(0), blk(0),
                  pl.BlockSpec((group, 1, rows), lambda h, r: (h, 0, r)),
                  blk(3 * groups),
                  pl.BlockSpec((1, HEAD_DIM), lambda h, r: (0, 0))],
        out_specs=blk(0),
        out_shape=jax.ShapeDtypeStruct((s, heads * HEAD_DIM), BF16),
        scratch_shapes=[pltpu.VMEM((group, HEAD_DIM, HEAD_DIM), F32)],
        compiler_params=_params("parallel", "arbitrary"),
        name="gdn_chunk",
    )(qkv, qkv, qkv, gcb, bb, gct.reshape(gct.shape[0], 1, s), proj, out_norm.reshape(1, HEAD_DIM))


def _rope(x, cos_ref, sin_a_ref, sin_b_ref):
    half = ROPE_DIM // 2
    return (x * cos_ref[...] + pltpu.roll(x, LANES - half, 1) * sin_a_ref[...]
            + pltpu.roll(x, half, 1) * sin_b_ref[...])


def _kv_down_body(a_ref, w1_ref, w2_ref, g1_ref, g2_ref, cos_ref, sa_ref, sb_ref, ckv_ref, kr_ref):
    a = a_ref[...]
    ckv = _dot(a, w1_ref[...].astype(BF16))
    ckv_ref[...] = _rms(ckv, g1_ref[...], ckv.shape[-1]).astype(ckv_ref.dtype)
    kr = _rope(_rms(_dot(a, w2_ref[...].astype(BF16)), g2_ref[...], ROPE_DIM), cos_ref, sa_ref, sb_ref)
    kr_ref[...] = (kr + pltpu.roll(kr, ROPE_DIM, 1)).astype(kr_ref.dtype)


def _kv_down(a, w_c, w_r, g_c, g_r, rope_tabs, tm=512):
    m, k = a.shape
    lat = w_c.shape[1]
    tm = min(tm, m)
    full = lambda shp: pl.BlockSpec(shp, lambda i: (0, 0))
    row = lambda w: pl.BlockSpec((tm, w), lambda i: (i, 0))
    return pl.pallas_call(
        _kv_down_body,
        grid=(m // tm,),
        in_specs=[row(k), full((k, lat)), full((k, LANES)), full((1, lat)), full((1, LANES)),
                  row(LANES), row(LANES), row(LANES)],
        out_specs=[row(lat), row(LANES)],
        out_shape=[jax.ShapeDtypeStruct((m, lat), BF16), jax.ShapeDtypeStruct((m, LANES), BF16)],
        compiler_params=_params("parallel"),
    )(a, w_c, w_r, g_c, g_r, *rope_tabs)


def _kv_up_body(a_ref, w_ref, kr_ref, g_ref, k_ref, v_ref):
    acc = _dot(a_ref[...], w_ref[...].astype(BF16))
    ones = jnp.ones((acc.shape[0], HEAD_DIM), v_ref.dtype)
    for h in range(acc.shape[1] // (2 * HEAD_DIM)):
        base = 2 * HEAD_DIM * h
        kn = acc[:, base:base + HEAD_DIM]
        k_ref[:, base:base + HEAD_DIM] = _rms(kn, g_ref[...], HEAD_DIM).astype(k_ref.dtype)
        k_ref[:, base + HEAD_DIM:base + 2 * HEAD_DIM] = kr_ref[...]
        v_ref[:, base:base + HEAD_DIM] = acc[:, base + HEAD_DIM:base + 2 * HEAD_DIM].astype(v_ref.dtype)
        v_ref[:, base + HEAD_DIM:base + 2 * HEAD_DIM] = ones


def _kv_up(ckv, w_ukv, kr, g_nope, tm=1024, hps=4):
    m, lat = ckv.shape
    n = w_ukv.shape[1]
    tm = min(tm, m)
    tn = hps * 2 * HEAD_DIM
    return pl.pallas_call(
        _kv_up_body,
        grid=(m // tm, n // tn),
        in_specs=[pl.BlockSpec((tm, lat), lambda i, j: (i, 0)),
                  pl.BlockSpec((lat, tn), lambda i, j: (0, j)),
                  pl.BlockSpec((tm, LANES), lambda i, j: (i, 0)),
                  pl.BlockSpec((1, HEAD_DIM), lambda i, j: (0, 0))],
        out_specs=[pl.BlockSpec((tm, tn), lambda i, j: (i, j)),
                   pl.BlockSpec((tm, tn), lambda i, j: (i, j))],
        out_shape=[jax.ShapeDtypeStruct((m, n), BF16), jax.ShapeDtypeStruct((m, n), BF16)],
        compiler_params=_params("parallel", "parallel"),
        name="mla_kv_up",
    )(ckv, w_ukv, kr, g_nope)


def _q_down_body(a_ref, w_ref, g_ref, o_ref):
    cq = _dot(a_ref[...], w_ref[...].astype(BF16))
    o_ref[...] = _rms(cq, g_ref[...], cq.shape[-1]).astype(o_ref.dtype)


def _q_down(a, w, gain, tm=512):
    m, k = a.shape
    n = w.shape[1]
    tm = min(tm, m)
    return pl.pallas_call(
        _q_down_body,
        grid=(m // tm,),
        in_specs=[pl.BlockSpec((tm, k), lambda i: (i, 0)),
                  pl.BlockSpec((k, n), lambda i: (0, 0)),
                  pl.BlockSpec((1, n), lambda i: (0, 0))],
        out_specs=pl.BlockSpec((tm, n), lambda i: (i, 0)),
        out_shape=jax.ShapeDtypeStruct((m, n), BF16),
        compiler_params=_params("parallel"),
    )(a, w, gain.reshape(1, n))


def _q_up_body(a_ref, w_ref, gn_ref, gr_ref, tab_ref, o_ref, *, scale):
    acc = _dot(a_ref[...], w_ref[...].astype(BF16))
    first = (lax.broadcasted_iota(jnp.int32, (1, LANES), 1) < ROPE_DIM).astype(F32)
    rope_gain = gr_ref[...] * tab_ref[...] * scale
    for h in range(acc.shape[1] // (2 * HEAD_DIM)):
        base = 2 * HEAD_DIM * h
        qn = _rms(acc[:, base:base + HEAD_DIM], gn_ref[...], HEAD_DIM)
        o_ref[:, base:base + HEAD_DIM] = (qn * scale).astype(o_ref.dtype)
        x = acc[:, base + HEAD_DIM:base + 2 * HEAD_DIM]
        ms = jnp.sum(x * x * first, axis=-1, keepdims=True) * (1.0 / ROPE_DIM)
        o_ref[:, base + HEAD_DIM:base + 2 * HEAD_DIM] = (x * lax.rsqrt(ms + RMS_EPS) * rope_gain).astype(o_ref.dtype)


def _q_up(cq, w, g_nope, g_rope, rope_tab, scale, tm=1024, hps=4):
    m, lat = cq.shape
    n = w.shape[1]
    tm = min(tm, m)
    tn = hps * 2 * HEAD_DIM
    vec = pl.BlockSpec((1, LANES), lambda i, j: (0, 0))
    return pl.pallas_call(
        functools.partial(_q_up_body, scale=scale),
        grid=(m // tm, n // tn),
        in_specs=[pl.BlockSpec((tm, lat), lambda i, j: (i, 0)),
                  pl.BlockSpec((lat, tn), lambda i, j: (0, j)),
                  vec, vec, pl.BlockSpec((tm, LANES), lambda i, j: (i, 0))],
        out_specs=pl.BlockSpec((tm, tn), lambda i, j: (i, j)),
        out_shape=jax.ShapeDtypeStruct((m, n), BF16),
        compiler_params=_params("parallel", "parallel"),
        name="mla_q_up",
    )(cq, w, g_nope, g_rope, rope_tab)


def _attn_body(q_ref, k_ref, v_ref, o_ref, s_ref):
    qi = pl.program_id(1)
    t = q_ref.shape[0]
    q = q_ref[...]

    def scores(j, slot):
        s_ref[slot] = _dot_nt(q, k_ref[pl.ds(pl.multiple_of(j * t, t), t), :])

    def update(j, slot, carry, masked):
        m, acc = carry
        s = s_ref[slot]
        if masked:
            row = lax.broadcasted_iota(jnp.int32, (t, t), 0)
            col = lax.broadcasted_iota(jnp.int32, (t, t), 1)
            s = jnp.where(col <= row, s, NEG_INF)
        m_new = jnp.maximum(m, jnp.max(s, axis=-1, keepdims=True))
        p = jnp.exp2(s - m_new)
        v = v_ref[pl.ds(pl.multiple_of(j * t, t), t), :]
        acc = jnp.exp2(m - m_new) * acc + _dot(p.astype(v.dtype), v)
        return m_new, acc

    def pair(i, carry):
        scores(2 * i + 1, 1)
        carry = update(2 * i, 0, carry, False)
        scores(2 * i + 2, 0)
        return update(2 * i + 1, 1, carry, False)

    def odd_tail(carry):
        scores(qi, 1)
        return update(qi, 1, update(qi - 1, 0, carry, False), True)

    def even_tail(carry):
        return update(qi, 0, carry, True)

    scores(0, 0)
    init = (jnp.full((t, 1), NEG_INF, F32), jnp.zeros((t, v_ref.shape[1]), F32))
    carry = lax.fori_loop(0, qi // 2, pair, init)
    _, acc = lax.cond(qi % 2 == 1, odd_tail, even_tail, carry)
    o_ref[...] = (acc[:, :HEAD_DIM] / acc[:, HEAD_DIM:HEAD_DIM + 1]).astype(o_ref.dtype)


def _attention(q_cat, k_cat, v, heads, t=1024):
    s = q_cat.shape[0]
    t = min(t, s)
    return pl.pallas_call(
        _attn_body,
        grid=(heads, s // t),
        in_specs=[pl.BlockSpec((t, 2 * HEAD_DIM), lambda h, i: (i, h)),
                  pl.BlockSpec((s, 2 * HEAD_DIM), lambda h, i: (0, h)),
                  pl.BlockSpec((s, 2 * HEAD_DIM), lambda h, i: (0, h))],
        out_specs=pl.BlockSpec((t, HEAD_DIM), lambda h, i: (i, h)),
        out_shape=jax.ShapeDtypeStruct((s, heads * HEAD_DIM), BF16),
        scratch_shapes=[pltpu.VMEM((2, t, t), F32)],
        compiler_params=_params("parallel", "parallel"),
        name="mla_attention",
    )(q_cat, k_cat, v)


def _router_body(h_ref, w_ref, b_ref, info_ref, cnt_ref, run_ref, *, experts):
    i = pl.program_id(0)

    @pl.when(i == 0)
    def _():
        run_ref[...] = jnp.zeros_like(run_ref)

    h = h_ref[...].astype(BF16)
    tm = h.shape[0]
    w = w_ref[...]
    w_hi = w.astype(BF16)
    rem = w - w_hi.astype(F32)
    w_mid = rem.astype(BF16)
    w_lo = (rem - w_mid.astype(F32)).astype(BF16)
    logits = _dot(h, w_hi) + _dot(h, w_mid) + _dot(h, w_lo) + b_ref[...]
    lane = lax.broadcasted_iota(jnp.int32, (tm, LANES), 1)
    logits = jnp.where(lane < experts, logits, NEG_INF)
    e = jnp.exp(logits - jnp.max(logits, axis=-1, keepdims=True))
    probs = e / jnp.sum(e, axis=-1, keepdims=True)
    probs = jnp.where(lane < experts, probs, -1.0)
    p1 = jnp.max(probs, axis=-1, keepdims=True)
    i1 = jnp.min(jnp.where(probs == p1, lane, LANES), axis=-1, keepdims=True)
    rest = jnp.where(lane == i1, -1.0, probs)
    p2 = jnp.max(rest, axis=-1, keepdims=True)
    i2 = jnp.min(jnp.where(rest == p2, lane, LANES), axis=-1, keepdims=True)
    total = p1 + p2
    hot1 = (lane == i1).astype(F32)
    hot2 = (lane == i2).astype(F32)
    hot = hot1 + hot2
    r = lax.broadcasted_iota(jnp.int32, (tm, tm), 0)
    c = lax.broadcasted_iota(jnp.int32, (tm, tm), 1)
    before = (c < r).astype(BF16)
    rank = _dot(before, hot.astype(BF16)) + run_ref[...]
    rank1 = jnp.sum(rank * hot1, axis=-1, keepdims=True)
    rank2 = jnp.sum(rank * hot2, axis=-1, keepdims=True)
    vals = (i1.astype(F32), i2.astype(F32), p1 / total, p2 / total, rank1, rank2)
    info = jnp.zeros((tm, LANES), F32)
    for idx, val in enumerate(vals):
        info = jnp.where(lane == idx, val, info)
    info_ref[...] = info
    run_ref[...] += jnp.sum(hot, axis=0, keepdims=True)
    cnt_ref[...] = run_ref[...]


def _router(h, w_pad, b_pad, experts, tm=512):
    s, d = h.shape
    tm = min(tm, s)
    return pl.pallas_call(
        functools.partial(_router_body, experts=experts),
        grid=(s // tm,),
        in_specs=[pl.BlockSpec((tm, d), lambda i: (i, 0)),
                  pl.BlockSpec((d, LANES), lambda i: (0, 0)),
                  pl.BlockSpec((1, LANES), lambda i: (0, 0))],
        out_specs=[pl.BlockSpec((tm, LANES), lambda i: (i, 0)),
                   pl.BlockSpec((1, LANES), lambda i: (0, 0))],
        out_shape=[jax.ShapeDtypeStruct((s, LANES), F32), jax.ShapeDtypeStruct((1, LANES), F32)],
        scratch_shapes=[pltpu.VMEM((1, LANES), F32)],
        compiler_params=_params("arbitrary"),
    )(h, w_pad, b_pad)


def _row_copy(src_ref, src_row, dst_ref, dst_row, sem):
    return pltpu.make_async_copy(src_ref.at[pl.ds(src_row, 1), :], dst_ref.at[pl.ds(dst_row, 1), :], sem)


def _dispatch_body(src_ref, h_ref, o_ref, buf_ref, sem):
    rows = o_ref.shape[0]
    base = pl.program_id(0) * rows

    def copy(n):
        return _row_copy(h_ref, src_ref[base + n], buf_ref, n, sem)

    def start(n, carry):
        copy(n).start()
        return carry

    def wait(n, carry):
        copy(n).wait()
        return carry

    lax.fori_loop(0, rows, start, 0, unroll=8)
    lax.fori_loop(0, rows, wait, 0, unroll=8)
    o_ref[...] = buf_ref[...].astype(o_ref.dtype)


def _dispatch(src, h, rows=512):
    d = h.shape[1]
    padded_rows = src.shape[0]
    return pl.pallas_call(
        _dispatch_body,
        grid_spec=pltpu.PrefetchScalarGridSpec(
            num_scalar_prefetch=1,
            grid=(padded_rows // rows,),
            in_specs=[pl.BlockSpec(memory_space=pl.ANY)],
            out_specs=pl.BlockSpec((rows, d), lambda i, src: (i, 0)),
            scratch_shapes=[pltpu.VMEM((rows, d), h.dtype), pltpu.SemaphoreType.DMA(())]),
        out_shape=jax.ShapeDtypeStruct((padded_rows, d), BF16),
        compiler_params=_params("arbitrary"),
        name="moe_dispatch",
    )(src, h)


MOE_STEP_SPARE, MOE_STEP_REUSE, MOE_STEP_NEW_WEIGHTS = 0, 1, 2


def _moe_up_body(tile_ref, col_ref, exp_ref, wcol_ref, kind_ref, a_ref, wg_ref, wu_ref, o_ref, wg16_ref, wu16_ref):
    del tile_ref, col_ref, exp_ref, wcol_ref
    kind = kind_ref[pl.program_id(0)]

    @pl.when(kind == MOE_STEP_NEW_WEIGHTS)
    def _():
        wg16_ref[...] = wg_ref[...].astype(BF16)
        wu16_ref[...] = wu_ref[...].astype(BF16)

    @pl.when(kind != MOE_STEP_SPARE)
    def _():
        a = a_ref[...]
        g = _dot(a, wg16_ref[...])
        u = _dot(a, wu16_ref[...])
        o_ref[...] = (_silu(g) * u).astype(o_ref.dtype)

    @pl.when(kind == MOE_STEP_SPARE)
    def _():
        o_ref[...] = jnp.zeros_like(o_ref)


def _moe_down_body(tile_ref, col_ref, exp_ref, wcol_ref, kind_ref, a_ref, w_ref, o_ref, w16_ref):
    del tile_ref, col_ref, exp_ref, wcol_ref
    kind = kind_ref[pl.program_id(0)]

    @pl.when(kind == MOE_STEP_NEW_WEIGHTS)
    def _():
        w16_ref[...] = w_ref[...].astype(BF16)

    @pl.when(kind != MOE_STEP_SPARE)
    def _():
        o_ref[...] = _dot(a_ref[...], w16_ref[...])

    @pl.when(kind == MOE_STEP_SPARE)
    def _():
        o_ref[...] = jnp.zeros_like(o_ref)


def _moe_matmul(body, name, sched, a, weights, out_dtype, tm, tn):
    rows = a.shape[0]
    _, k, n = weights[0].shape
    wspec = pl.BlockSpec((None, k, tn), lambda s, tile, col, exp, wcol, valid: (exp[s], 0, wcol[s]))
    return pl.pallas_call(
        body,
        grid_spec=pltpu.PrefetchScalarGridSpec(
            num_scalar_prefetch=5,
            grid=(sched[0].shape[0],),
            in_specs=[pl.BlockSpec((tm, a.shape[1]), lambda s, tile, col, exp, wcol, valid: (tile[s], 0))]
            + [wspec] * len(weights),
            out_specs=pl.BlockSpec((tm, tn), lambda s, tile, col, exp, wcol, valid: (tile[s], col[s])),
            scratch_shapes=[pltpu.VMEM((k, tn), BF16)] * len(weights)),
        out_shape=jax.ShapeDtypeStruct((rows, n), out_dtype),
        compiler_params=_params("arbitrary"),
        name=name,
    )(*sched, a, *weights)


def _combine_body(slot_ref, y_ref, x_ref, info_ref, gate_ref, o_ref, buf_ref, sem, *, tokens):
    base = pl.program_id(0) * tokens

    def copy(n):
        return _row_copy(y_ref, slot_ref[base * MOE_TOP_K + n], buf_ref.at[n % MOE_TOP_K], n // MOE_TOP_K, sem)

    def start(n, carry):
        copy(n).start()
        return carry

    def wait(n, carry):
        copy(n).wait()
        return carry

    lax.fori_loop(0, tokens * MOE_TOP_K, start, 0, unroll=8)
    lax.fori_loop(0, tokens * MOE_TOP_K, wait, 0, unroll=8)
    info = info_ref[...]
    mix = info[:, 2:3] * buf_ref[0] + info[:, 3:4] * buf_ref[1]
    o_ref[...] = x_ref[...] + gate_ref[...] * mix


def _combine(slots, y, x, info, gate, tokens=256):
    s, d = x.shape
    tokens = min(tokens, s)
    return pl.pallas_call(
        functools.partial(_combine_body, tokens=tokens),
        grid_spec=pltpu.PrefetchScalarGridSpec(
            num_scalar_prefetch=1,
            grid=(s // tokens,),
            in_specs=[pl.BlockSpec(memory_space=pl.ANY),
                      pl.BlockSpec((tokens, d), lambda i, slot: (i, 0)),
                      pl.BlockSpec((tokens, LANES), lambda i, slot: (i, 0)),
                      pl.BlockSpec((1, d), lambda i, slot: (0, 0))],
            out_specs=pl.BlockSpec((tokens, d), lambda i, slot: (i, 0)),
            scratch_shapes=[pltpu.VMEM((MOE_TOP_K, tokens, d), F32), pltpu.SemaphoreType.DMA(())]),
        out_shape=jax.ShapeDtypeStruct((s, d), F32),
        compiler_params=_params("arbitrary"),
    )(slots, y, x, info, gate.reshape(1, d))


def _moe_schedule(info, counts, experts, tm, n_col):
    s = info.shape[0]
    max_tiles = (s * MOE_TOP_K) // tm + experts
    ids = info[:, 0:MOE_TOP_K].astype(jnp.int32)
    ranks = info[:, 4:4 + MOE_TOP_K].astype(jnp.int32)
    cnt = counts[0, :experts].astype(jnp.int32)
    tiles = (cnt + tm - 1) // tm
    tile_end = jnp.cumsum(tiles)
    tile_start = tile_end - tiles
    slots = ((tile_start * tm)[ids] + ranks).reshape(-1)
    token = jnp.arange(s * MOE_TOP_K, dtype=jnp.int32) // MOE_TOP_K
    src = jnp.zeros((max_tiles * tm,), jnp.int32).at[slots].set(token)
    step_end = tile_end * n_col
    total = step_end[-1]
    steps = jnp.arange(max_tiles * n_col, dtype=jnp.int32)
    valid = steps < total
    st = jnp.minimum(steps, total - 1)
    exp = jnp.sum(st[:, None] >= step_end[None, :], axis=1)
    local = st - (step_end - tiles * n_col)[exp]
    wcol = local // tiles[exp]
    spare = steps - total
    col = jnp.where(valid, wcol, spare % n_col)
    tile = jnp.where(valid, tile_start[exp] + local % tiles[exp], tile_end[-1] + spare // n_col)
    first = (local % tiles[exp]) == 0
    kind = jnp.where(valid, jnp.where(first, MOE_STEP_NEW_WEIGHTS, MOE_STEP_REUSE), MOE_STEP_SPARE)
    sched = tuple(v.astype(jnp.int32) for v in (tile, col, exp, wcol, kind))
    return slots.astype(jnp.int32), src, sched


def _rope_tables(positions):
    inv_freq = ROPE_THETA ** (-jnp.arange(0, ROPE_DIM, 2, dtype=F32) / ROPE_DIM)
    ang = positions.astype(F32)[:, None] * inv_freq
    cos, sin = jnp.cos(ang), jnp.sin(ang)
    zero = jnp.zeros_like(cos)
    pad = jnp.zeros((cos.shape[0], LANES - ROPE_DIM), F32)
    key_tabs = (jnp.concatenate([cos, cos, pad], axis=1),
                jnp.concatenate([-sin, zero, pad], axis=1),
                jnp.concatenate([zero, sin, pad], axis=1))
    return key_tabs, jnp.concatenate([cos, cos, -sin, sin], axis=1)


def _pad_lanes(v):
    return jnp.pad(v, ((0, 0), (0, LANES - v.shape[1])))


def kernel(x, c, positions, ada_w, ada_b, norm_mix, norm_ffn, gdn_w_in, gdn_conv, gdn_a_log, gdn_dt_bias, gdn_out_norm, gdn_w_out, kv_norm, w_dkv, kv_latent_norm, w_ukv, k_nope_norm, k_rope_norm, mla_w_dq, mla_q_latent_norm, mla_w_uq, mla_q_nope_norm, mla_q_rope_norm, mla_w_out, ffn_w_gate, ffn_w_up, ffn_w_down, router_w, router_b, moe_w_gate, moe_w_up, moe_w_down):
    assert x.shape[0] == 1 and ada_w.shape[0] == 2
    _, s, d = x.shape
    heads = d // HEAD_DIM
    width = heads * HEAD_DIM
    x0 = x[0]

    mod = _ada_mod(c, ada_w, ada_b).reshape(2, ADA_CHUNKS, d)
    rope_tabs, q_rope_tab = _rope_tables(positions[0])

    sh_m, sc_m, g_m, sh_f, sc_f, g_f = (mod[0, n] for n in range(ADA_CHUNKS))
    h = _norm_mod(x0, norm_mix[0], sc_m, sh_m, name="norm_mix0")
    proj = _matmul_ws(h, gdn_w_in[0], 4 * width, name="gdn_in_proj", out_dtype=BF16)
    w_ba = gdn_w_in[0, :, 4 * width:].astype(BF16)
    b_raw = _matmul(h, _pad_lanes(w_ba[:, :heads]), name="gdn_beta_proj", out_dtype=F32)
    a_raw = _matmul(h, _pad_lanes(w_ba[:, heads:]), name="gdn_decay_proj", out_dtype=F32)
    qkv = _gdn_conv(proj, gdn_conv[0], width)
    gcb, bb, gct = _gdn_gate(b_raw, a_raw, _pad_lanes(gdn_a_log[0][None]), _pad_lanes(gdn_dt_bias[0][None]), width)
    o = _gdn_chunk(qkv, proj, gcb, bb, gct[:heads], gdn_out_norm[0], heads)
    x1 = _matmul(o, gdn_w_out[0].astype(BF16), name="gdn_out_proj", out_dtype=F32, res=x0, gate=g_m)

    h = _norm_mod(x1, norm_ffn[0], sc_f, sh_f, name="norm_ffn0")
    hid = _swiglu_up(h, ffn_w_gate[0].astype(BF16), ffn_w_up[0].astype(BF16))
    ffn_dim = hid.shape[1]
    x2 = _matmul_ktiled_res(hid, ffn_w_down[0].astype(BF16), x1, g_f, tk=ffn_dim // 2)

    lat = kv_latent_norm.shape[0]
    zeros = jnp.zeros((d,), F32)
    h_kv = _norm_mod(x2, kv_norm, zeros, zeros, name="norm_kv", modulate=False)
    w_dkv16 = w_dkv.astype(BF16)
    ckv, k_rope = _kv_down(h_kv, w_dkv16[:, :lat], _pad_lanes(w_dkv16[:, lat:]), kv_latent_norm[None],
                           _pad_lanes(k_rope_norm[None]), rope_tabs)
    k_cat, v = _kv_up(ckv, w_ukv.astype(BF16), k_rope, k_nope_norm[None])

    sh_m, sc_m, g_m, sh_f, sc_f, g_f = (mod[1, n] for n in range(ADA_CHUNKS))
    h = _norm_mod(x2, norm_mix[1], sc_m, sh_m, name="norm_mix1")
    cq = _q_down(h, mla_w_dq[0].astype(BF16), mla_q_latent_norm[0])
    q_lora = cq.shape[1]
    w_uq = mla_w_uq[0].astype(BF16).reshape(q_lora, heads, HEAD_DIM + ROPE_DIM)
    half = ROPE_DIM // 2
    swap = lambda t: jnp.concatenate([t[..., half:], t[..., :half]], axis=-1)
    w_rope = w_uq[..., HEAD_DIM:]
    w_uq = jnp.concatenate([w_uq, swap(w_rope)], axis=-1).reshape(q_lora, heads * 2 * HEAD_DIM)
    g_rope = mla_q_rope_norm[0][None]
    q_cat = _q_up(cq, w_uq, mla_q_nope_norm[0][None], jnp.concatenate([g_rope, swap(g_rope)], axis=1),
                  q_rope_tab, (HEAD_DIM + ROPE_DIM) ** -0.5 * math.log2(math.e))
    o = _attention(q_cat, k_cat, v, heads)
    x3 = _matmul(o, mla_w_out[0].astype(BF16), name="mla_out_proj", out_dtype=F32, res=x2, gate=g_m)

    experts = router_w.shape[-1]
    moe_tm, moe_tn = 512, 512
    assert moe_w_gate.shape[-1] == d
    h = _norm_mod(x3, norm_ffn[1], sc_f, sh_f, name="norm_ffn1", out_dtype=F32)
    info, counts = _router(h, _pad_lanes(router_w[0]), _pad_lanes(router_b[0][None]), experts)
    slots, src, sched = _moe_schedule(info, counts, experts, moe_tm, d // moe_tn)
    sorted_h = _dispatch(src, h, moe_tm)
    hid = _moe_matmul(_moe_up_body, "moe_up", sched, sorted_h, (moe_w_gate[0], moe_w_up[0]), BF16, moe_tm, moe_tn)
    y = _moe_matmul(_moe_down_body, "moe_down", sched, hid, (moe_w_down[0],), F32, moe_tm, moe_tn)
    out = _combine(slots, y, x3, info, g_f)
    return out[None]
```

```python
import functools
import math

import jax
import jax.numpy as jnp
from jax import lax
from jax.experimental import pallas as pl
from jax.experimental.pallas import tpu as pltpu

F32 = jnp.float32
BF16 = jnp.bfloat16

RMS_EPS = 1e-6
L2_EPS = 1e-6
NEG_INF = -1e30
ROPE_THETA = 10000.0

LANES = 128
HEAD_DIM = 128
ROPE_DIM = 64
GDN_CHUNK = 64
GDN_CONV = 4
MOE_TOP_K = 2
ADA_CHUNKS = 6
VMEM_LIMIT_BYTES = 56 * 1024 * 1024


def _params(*semantics):
    return pltpu.CompilerParams(dimension_semantics=semantics, vmem_limit_bytes=VMEM_LIMIT_BYTES)


def _dot(a, b):
    return jnp.dot(a, b, preferred_element_type=F32)


def _dot_nt(a, b):
    return lax.dot_general(a, b, (((1,), (1,)), ((), ())), preferred_element_type=F32)


def _dot_tn(a, b):
    return lax.dot_general(a, b, (((0,), (0,)), ((), ())), preferred_element_type=F32)


def _silu(x):
    return x * jax.nn.sigmoid(x)


def _split3(x):
    hi = x.astype(BF16)
    rem = x - hi.astype(F32)
    mid = rem.astype(BF16)
    return hi, mid, (rem - mid.astype(F32)).astype(BF16)


def _rms(x, gain, n):
    ms = jnp.sum(x * x, axis=-1, keepdims=True) * (1.0 / n)
    return x * lax.rsqrt(ms + RMS_EPS) * gain


def _ada_body(c_ref, w_ref, b_ref, o_ref):
    cs = _silu(c_ref[...])
    for j in range(o_ref.shape[-1] // LANES):
        sl = slice(j * LANES, (j + 1) * LANES)
        s = jnp.sum(w_ref[0, :, sl] * cs, axis=0, keepdims=True)
        o_ref[0, :, sl] = s + b_ref[0, :, sl]


def _ada_mod(c, ada_w, ada_b, tn=1024):
    depth, d, n = ada_w.shape
    c_b = jnp.broadcast_to(c.reshape(d, 1), (d, LANES))
    return pl.pallas_call(
        _ada_body,
        grid=(depth, n // tn),
        in_specs=[pl.BlockSpec((d, LANES), lambda l, j: (0, 0)),
                  pl.BlockSpec((1, d, tn), lambda l, j: (l, 0, j)),
                  pl.BlockSpec((1, 1, tn), lambda l, j: (l, 0, j))],
        out_specs=pl.BlockSpec((1, 1, tn), lambda l, j: (l, 0, j)),
        out_shape=jax.ShapeDtypeStruct((depth, 1, n), F32),
        compiler_params=_params("parallel", "parallel"),
    )(c_b, ada_w, ada_b.reshape(depth, 1, n))


def _norm_body(x_ref, g_ref, sc_ref, sh_ref, o_ref, *, modulate):
    x = x_ref[...]
    y = _rms(x, g_ref[...], x.shape[-1])
    if modulate:
        y = y * (1.0 + sc_ref[...]) + sh_ref[...]
    o_ref[...] = y.astype(o_ref.dtype)


def _norm_mod(x, gain, scale, shift, *, name, modulate=True, out_dtype=BF16, tr=256):
    s, d = x.shape
    tr = min(tr, s)
    vec = pl.BlockSpec((1, d), lambda i: (0, 0))
    return pl.pallas_call(
        functools.partial(_norm_body, modulate=modulate),
        grid=(s // tr,),
        in_specs=[pl.BlockSpec((tr, d), lambda i: (i, 0)), vec, vec, vec],
        out_specs=pl.BlockSpec((tr, d), lambda i: (i, 0)),
        out_shape=jax.ShapeDtypeStruct((s, d), out_dtype),
        compiler_params=_params("parallel"),
        name=name,
    )(x, gain.reshape(1, d), scale.reshape(1, d), shift.reshape(1, d))


def _mm_body(a_ref, w_ref, o_ref):
    o_ref[...] = _dot(a_ref[...], w_ref[...].astype(BF16)).astype(o_ref.dtype)


def _mm_res_body(a_ref, w_ref, r_ref, g_ref, o_ref):
    o_ref[...] = r_ref[...] + g_ref[...] * _dot(a_ref[...], w_ref[...].astype(BF16))


def _matmul(a, w, *, name, out_dtype, tm=1024, tn=512, res=None, gate=None, n_cols=None):
    m, k = a.shape
    n = w.shape[1] if n_cols is None else n_cols
    tm, tn = min(tm, m), min(tn, n)
    in_specs = [pl.BlockSpec((tm, k), lambda i, j: (i, 0)),
                pl.BlockSpec((k, tn), lambda i, j: (0, j))]
    args = [a, w]
    body = _mm_body
    if res is not None:
        in_specs += [pl.BlockSpec((tm, tn), lambda i, j: (i, j)),
                     pl.BlockSpec((1, tn), lambda i, j: (0, j))]
        args += [res, gate.reshape(1, n)]
        body = _mm_res_body
    return pl.pallas_call(
        body,
        grid=(m // tm, n // tn),
        in_specs=in_specs,
        out_specs=pl.BlockSpec((tm, tn), lambda i, j: (i, j)),
        out_shape=jax.ShapeDtypeStruct((m, n), out_dtype),
        compiler_params=_params("parallel", "parallel"),
        name=name,
    )(*args)


def _mm_ktiled_res_body(a_ref, w_ref, r_ref, g_ref, o_ref, acc_ref):
    kk = pl.program_id(2)

    @pl.when(kk == 0)
    def _():
        acc_ref[...] = jnp.zeros_like(acc_ref)

    acc_ref[...] += _dot(a_ref[...], w_ref[...].astype(BF16))

    @pl.when(kk == pl.num_programs(2) - 1)
    def _():
        o_ref[...] = r_ref[...] + g_ref[...] * acc_ref[...]


def _matmul_ktiled_res(a, w, res, gate, *, tm=512, tn=1024, tk):
    m, k = a.shape
    n = w.shape[1]
    tm, tn = min(tm, m), min(tn, n)
    return pl.pallas_call(
        _mm_ktiled_res_body,
        grid=(m // tm, n // tn, k // tk),
        in_specs=[pl.BlockSpec((tm, tk), lambda i, j, kk: (i, kk)),
                  pl.BlockSpec((tk, tn), lambda i, j, kk: (kk, j)),
                  pl.BlockSpec((tm, tn), lambda i, j, kk: (i, j)),
                  pl.BlockSpec((1, tn), lambda i, j, kk: (0, j))],
        out_specs=pl.BlockSpec((tm, tn), lambda i, j, kk: (i, j)),
        out_shape=jax.ShapeDtypeStruct((m, n), F32),
        scratch_shapes=[pltpu.VMEM((tm, tn), F32)],
        compiler_params=_params("parallel", "parallel", "arbitrary"),
    )(a, w, res, gate.reshape(1, n))


def _swiglu_up_body(a_ref, wg_ref, wu_ref, o_ref):
    a = a_ref[...]
    g = _dot(a, wg_ref[...].astype(BF16))
    u = _dot(a, wu_ref[...].astype(BF16))
    o_ref[...] = (_silu(g) * u).astype(o_ref.dtype)


def _swiglu_up(a, wg, wu, *, tm=1024, tn=256):
    m, k = a.shape
    n = wg.shape[1]
    tm = min(tm, m)
    wspec = pl.BlockSpec((k, tn), lambda i, j: (0, j))
    return pl.pallas_call(
        _swiglu_up_body,
        grid=(m // tm, n // tn),
        in_specs=[pl.BlockSpec((tm, k), lambda i, j: (i, 0)), wspec, wspec],
        out_specs=pl.BlockSpec((tm, tn), lambda i, j: (i, j)),
        out_shape=jax.ShapeDtypeStruct((m, n), BF16),
        compiler_params=_params("parallel", "parallel"),
    )(a, wg, wu)


def _gdn_conv_body(prev_ref, x_ref, w_ref, o_ref):
    i = pl.program_id(0)
    which = pl.program_id(1)
    tr = x_ref.shape[0]
    x16 = x_ref[...]
    prev = prev_ref[...]
    rows2 = jnp.concatenate([jnp.where(i > 0, prev, jnp.zeros_like(prev)), x16], axis=0)
    r = lax.broadcasted_iota(jnp.int32, (tr, 2 * tr), 0)
    c = lax.broadcasted_iota(jnp.int32, (tr, 2 * tr), 1)
    taps = [_dot((c == r + tr - d).astype(BF16), rows2) for d in range(GDN_CONV - 1, 0, -1)]
    for h in range(x_ref.shape[1] // HEAD_DIM):
        sl = slice(h * HEAD_DIM, (h + 1) * HEAD_DIM)
        y = taps[0][:, sl] * w_ref[0:1, sl]
        for j in range(1, GDN_CONV - 1):
            y = y + taps[j][:, sl] * w_ref[j:j + 1, sl]
        y = y + x16[:, sl].astype(F32) * w_ref[GDN_CONV - 1:GDN_CONV, sl]
        y = _silu(y)
        ss = jnp.sum(y * y, axis=-1, keepdims=True)
        yn = y * lax.rsqrt(ss + L2_EPS)
        o_ref[:, sl] = jnp.where(which < 2, yn, y).astype(o_ref.dtype)


def _gdn_conv(proj, conv_w, width, tr=128):
    s = proj.shape[0]
    tr = min(tr, s)
    return pl.pallas_call(
        _gdn_conv_body,
        grid=(s // tr, 3),
        in_specs=[pl.BlockSpec((tr, width), lambda i, c: (jnp.maximum(i - 1, 0), c)),
                  pl.BlockSpec((tr, width), lambda i, c: (i, c)),
                  pl.BlockSpec((GDN_CONV, width), lambda i, c: (0, c))],
        out_specs=pl.BlockSpec((tr, width), lambda i, c: (i, c)),
        out_shape=jax.ShapeDtypeStruct((s, 3 * width), BF16),
        compiler_params=_params("parallel", "parallel"),
        name="gdn_conv",
    )(proj, proj, conv_w)


def _gdn_gate_body(b_ref, a_ref, alog_ref, dtb_ref, gcb_ref, bb_ref, gct_ref):
    tr = b_ref.shape[0]
    beta = jax.nn.sigmoid(b_ref[...])
    x = a_ref[...] + dtb_ref[...]
    softplus = jnp.maximum(x, 0.0) + jnp.log1p(jnp.exp(-jnp.abs(x)))
    g = -jnp.exp(alog_ref[...]) * softplus
    r = lax.broadcasted_iota(jnp.int32, (tr, tr), 0)
    c = lax.broadcasted_iota(jnp.int32, (tr, tr), 1)
    shift = int(math.log2(GDN_CHUNK))
    tri = ((r >> shift == c >> shift) & (c <= r)).astype(BF16)
    gc = sum(_dot(tri, part) for part in _split3(g))
    width = gcb_ref.shape[1]
    er = lax.broadcasted_iota(jnp.int32, (LANES, width), 0)
    ec = lax.broadcasted_iota(jnp.int32, (LANES, width), 1)
    expand = (ec >> int(math.log2(HEAD_DIM)) == er).astype(BF16)
    gc_parts = _split3(gc)
    gcb_ref[...] = sum(_dot(part, expand) for part in gc_parts)
    bb_ref[...] = sum(_dot(part, expand) for part in _split3(beta))
    ir = lax.broadcasted_iota(jnp.int32, (LANES, LANES), 0)
    ic = lax.broadcasted_iota(jnp.int32, (LANES, LANES), 1)
    eye = (ir == ic).astype(BF16)
    gct_ref[...] = sum(_dot_nt(eye, part) for part in gc_parts)


def _gdn_gate(b_raw, a_raw, a_log, dt_bias, width, tr=512):
    s = b_raw.shape[0]
    tr = min(tr, s)
    row = pl.BlockSpec((tr, LANES), lambda i: (i, 0))
    vec = pl.BlockSpec((1, LANES), lambda i: (0, 0))
    wide = pl.BlockSpec((tr, width), lambda i: (i, 0))
    return pl.pallas_call(
        _gdn_gate_body,
        grid=(s // tr,),
        in_specs=[row, row, vec, vec],
        out_specs=[wide, wide, pl.BlockSpec((LANES, tr), lambda i: (0, i))],
        out_shape=[jax.ShapeDtypeStruct((s, width), F32), jax.ShapeDtypeStruct((s, width), F32),
                   jax.ShapeDtypeStruct((LANES, s), F32)],
        compiler_params=_params("parallel"),
    )(b_raw, a_raw, a_log, dt_bias)


def _gdn_chunk_body(q_ref, k_ref, v_ref, gcb_ref, bb_ref, gct_ref, z_ref, gain_ref, o_ref, state_ref):
    @pl.when(pl.program_id(1) == 0)
    def _():
        state_ref[...] = jnp.zeros_like(state_ref)

    cs = GDN_CHUNK
    n_heads = state_ref.shape[0]
    n_chunks = q_ref.shape[0] // cs
    probs = [(j, n) for j in range(n_heads) for n in range(n_chunks)]

    def tile(ref, j, n):
        return ref[n * cs:(n + 1) * cs, j * HEAD_DIM:(j + 1) * HEAD_DIM]

    r = lax.broadcasted_iota(jnp.int32, (cs, cs), 0)
    c = lax.broadcasted_iota(jnp.int32, (cs, cs), 1)
    causal = c <= r
    strict = c < r
    eye = (r == c).astype(F32)

    k = [tile(k_ref, j, n).astype(F32) for j, n in probs]
    k16 = [x.astype(BF16) for x in k]
    gc = [tile(gcb_ref, j, n) for j, n in probs]
    beta = [tile(bb_ref, j, n) for j, n in probs]
    kb = [x * b for x, b in zip(k, beta)]
    decay = []
    for (j, n), g in zip(probs, gc):
        gdiff = g[:, :cs] - gct_ref[j, :, n * cs:(n + 1) * cs]
        decay.append(jnp.where(causal, jnp.exp(jnp.where(causal, gdiff, 0.0)), 0.0))
    q = [tile(q_ref, j, n).astype(F32) * (HEAD_DIM ** -0.5) for j, n in probs]
    kq = [_dot_nt(jnp.concatenate([x.astype(BF16), y.astype(BF16)], axis=0), z) for x, y, z in zip(kb, q, k16)]
    p = [jnp.where(strict, -(x[:cs] * d), 0.0) for x, d in zip(kq, decay)]
    inv = [eye + x for x in p]
    for _ in range(int(math.log2(cs)) - 1):
        p16 = [x.astype(BF16) for x in p]
        p = [_dot(x, x) for x in p16]
        inv = [x + _dot(x.astype(BF16), y.astype(BF16)) for x, y in zip(inv, p)]
    t16 = [x.astype(BF16) for x in inv]
    uw = [_dot(t, jnp.concatenate([(tile(v_ref, j, n).astype(F32) * b).astype(BF16),
                                   (x * jnp.exp(g)).astype(BF16)], axis=1))
          for t, (j, n), b, x, g in zip(t16, probs, beta, kb, gc)]
    qk = [(x[cs:] * d).astype(BF16) for x, d in zip(kq, decay)]
    wq = [jnp.concatenate([x[:, HEAD_DIM:].astype(BF16), (y * jnp.exp(g)).astype(BF16)], axis=0)
          for x, y, g in zip(uw, q, gc)]
    g_last = [g[cs - 1:cs, :] for g in gc]
    k_end = [(x * jnp.exp(gl - g)).astype(BF16) for x, gl, g in zip(k, g_last, gc)]

    state = [state_ref[j] for j in range(n_heads)]
    for n in range(n_chunks):
        idx = [j * n_chunks + n for j in range(n_heads)]
        ws = [_dot(wq[i], s.astype(BF16)) for i, s in zip(idx, state)]
        v16 = [(uw[i][:, :HEAD_DIM] - x[:cs]).astype(BF16) for i, x in zip(idx, ws)]
        o = [x[cs:] + _dot(qk[i], v) for i, x, v in zip(idx, ws, v16)]
        state = [s * jnp.exp(g_last[i]) + _dot_tn(k_end[i], v) for i, s, v in zip(idx, state, v16)]
        for j in range(n_heads):
            z = tile(z_ref, j, n).astype(F32)
            o_ref[n * cs:(n + 1) * cs, j * HEAD_DIM:(j + 1) * HEAD_DIM] = (
                _rms(o[j], gain_ref[...], HEAD_DIM) * _silu(z)).astype(o_ref.dtype)
    for j in range(n_heads):
        state_ref[j] = state[j]


def _gdn_chunk(qkv, proj, gcb, bb, gct, out_norm, heads, rows=256, group=8):
    s = qkv.shape[0]
    rows = min(rows, s)
    groups = heads // group
    blk = lambda off: pl.BlockSpec((rows, group * HEAD_DIM), lambda h, r: (r, off + h))
    return pl.pallas_call(
        _gdn_chunk_body,
        grid=(groups, s // rows),
        in_specs=[blk(0), blk(groups), blk(2 * groups), blk(0), blk(0),
                  pl.BlockSpec((group, 1, rows), lambda h, r: (h, 0, r)),
                  blk(3 * groups),
                  pl.BlockSpec((1, HEAD_DIM), lambda h, r: (0, 0))],
        out_specs=blk(0),
        out_shape=jax.ShapeDtypeStruct((s, heads * HEAD_DIM), BF16),
        scratch_shapes=[pltpu.VMEM((group, HEAD_DIM, HEAD_DIM), F32)],
        compiler_params=_params("parallel", "arbitrary"),
        name="gdn_chunk",
    )(qkv, qkv, qkv, gcb, bb, gct.reshape(gct.shape[0], 1, s), proj, out_norm.reshape(1, HEAD_DIM))


def _rope(x, cos_ref, sin_a_ref, sin_b_ref):
    half = ROPE_DIM // 2
    return (x * cos_ref[...] + pltpu.roll(x, LANES - half, 1) * sin_a_ref[...]
            + pltpu.roll(x, half, 1) * sin_b_ref[...])


def _kv_down_body(a_ref, w1_ref, w2_ref, g1_ref, g2_ref, cos_ref, sa_ref, sb_ref, ckv_ref, kr_ref):
    a = a_ref[...]
    ckv = _dot(a, w1_ref[...].astype(BF16))
    ckv_ref[...] = _rms(ckv, g1_ref[...], ckv.shape[-1]).astype(ckv_ref.dtype)
    kr = _rope(_rms(_dot(a, w2_ref[...].astype(BF16)), g2_ref[...], ROPE_DIM), cos_ref, sa_ref, sb_ref)
    kr_ref[...] = (kr + pltpu.roll(kr, ROPE_DIM, 1)).astype(kr_ref.dtype)


def _kv_down(a, w_c, w_r, g_c, g_r, rope_tabs, tm=512):
    m, k = a.shape
    lat = w_c.shape[1]
    tm = min(tm, m)
    full = lambda shp: pl.BlockSpec(shp, lambda i: (0, 0))
    row = lambda w: pl.BlockSpec((tm, w), lambda i: (i, 0))
    return pl.pallas_call(
        _kv_down_body,
        grid=(m // tm,),
        in_specs=[row(k), full((k, lat)), full((k, LANES)), full((1, lat)), full((1, LANES)),
                  row(LANES), row(LANES), row(LANES)],
        out_specs=[row(lat), row(LANES)],
        out_shape=[jax.ShapeDtypeStruct((m, lat), BF16), jax.ShapeDtypeStruct((m, LANES), BF16)],
        compiler_params=_params("parallel"),
    )(a, w_c, w_r, g_c, g_r, *rope_tabs)


def _kv_up_body(a_ref, w_ref, kr_ref, g_ref, k_ref, v_ref):
    acc = _dot(a_ref[...], w_ref[...].astype(BF16))
    ones = jnp.ones((acc.shape[0], HEAD_DIM), v_ref.dtype)
    for h in range(acc.shape[1] // (2 * HEAD_DIM)):
        base = 2 * HEAD_DIM * h
        kn = acc[:, base:base + HEAD_DIM]
        k_ref[:, base:base + HEAD_DIM] = _rms(kn, g_ref[...], HEAD_DIM).astype(k_ref.dtype)
        k_ref[:, base + HEAD_DIM:base + 2 * HEAD_DIM] = kr_ref[...]
        v_ref[:, base:base + HEAD_DIM] = acc[:, base + HEAD_DIM:base + 2 * HEAD_DIM].astype(v_ref.dtype)
        v_ref[:, base + HEAD_DIM:base + 2 * HEAD_DIM] = ones


def _kv_up(ckv, w_ukv, kr, g_nope, tm=1024, hps=4):
    m, lat = ckv.shape
    n = w_ukv.shape[1]
    tm = min(tm, m)
    tn = hps * 2 * HEAD_DIM
    return pl.pallas_call(
        _kv_up_body,
        grid=(m // tm, n // tn),
        in_specs=[pl.BlockSpec((tm, lat), lambda i, j: (i, 0)),
                  pl.BlockSpec((lat, tn), lambda i, j: (0, j)),
                  pl.BlockSpec((tm, LANES), lambda i, j: (i, 0)),
                  pl.BlockSpec((1, HEAD_DIM), lambda i, j: (0, 0))],
        out_specs=[pl.BlockSpec((tm, tn), lambda i, j: (i, j)),
                   pl.BlockSpec((tm, tn), lambda i, j: (i, j))],
        out_shape=[jax.ShapeDtypeStruct((m, n), BF16), jax.ShapeDtypeStruct((m, n), BF16)],
        compiler_params=_params("parallel", "parallel"),
        name="mla_kv_up",
    )(ckv, w_ukv, kr, g_nope)


def _q_down_body(a_ref, w_ref, g_ref, o_ref):
    cq = _dot(a_ref[...], w_ref[...].astype(BF16))
    o_ref[...] = _rms(cq, g_ref[...], cq.shape[-1]).astype(o_ref.dtype)


def _q_down(a, w, gain, tm=512):
    m, k = a.shape
    n = w.shape[1]
    tm = min(tm, m)
    return pl.pallas_call(
        _q_down_body,
        grid=(m // tm,),
        in_specs=[pl.BlockSpec((tm, k), lambda i: (i, 0)),
                  pl.BlockSpec((k, n), lambda i: (0, 0)),
                  pl.BlockSpec((1, n), lambda i: (0, 0))],
        out_specs=pl.BlockSpec((tm, n), lambda i: (i, 0)),
        out_shape=jax.ShapeDtypeStruct((m, n), BF16),
        compiler_params=_params("parallel"),
    )(a, w, gain.reshape(1, n))


def _q_up_body(a_ref, w_ref, gn_ref, gr_ref, tab_ref, o_ref, *, scale):
    acc = _dot(a_ref[...], w_ref[...].astype(BF16))
    first = (lax.broadcasted_iota(jnp.int32, (1, LANES), 1) < ROPE_DIM).astype(F32)
    rope_gain = gr_ref[...] * tab_ref[...] * scale
    for h in range(acc.shape[1] // (2 * HEAD_DIM)):
        base = 2 * HEAD_DIM * h
        qn = _rms(acc[:, base:base + HEAD_DIM], gn_ref[...], HEAD_DIM)
        o_ref[:, base:base + HEAD_DIM] = (qn * scale).astype(o_ref.dtype)
        x = acc[:, base + HEAD_DIM:base + 2 * HEAD_DIM]
        ms = jnp.sum(x * x * first, axis=-1, keepdims=True) * (1.0 / ROPE_DIM)
        o_ref[:, base + HEAD_DIM:base + 2 * HEAD_DIM] = (x * lax.rsqrt(ms + RMS_EPS) * rope_gain).astype(o_ref.dtype)


def _q_up(cq, w, g_nope, g_rope, rope_tab, scale, tm=1024, hps=4):
    m, lat = cq.shape
    n = w.shape[1]
    tm = min(tm, m)
    tn = hps * 2 * HEAD_DIM
    vec = pl.BlockSpec((1, LANES), lambda i, j: (0, 0))
    return pl.pallas_call(
        functools.partial(_q_up_body, scale=scale),
        grid=(m // tm, n // tn),
        in_specs=[pl.BlockSpec((tm, lat), lambda i, j: (i, 0)),
                  pl.BlockSpec((lat, tn), lambda i, j: (0, j)),
                  vec, vec, pl.BlockSpec((tm, LANES), lambda i, j: (i, 0))],
        out_specs=pl.BlockSpec((tm, tn), lambda i, j: (i, j)),
        out_shape=jax.ShapeDtypeStruct((m, n), BF16),
        compiler_params=_params("parallel", "parallel"),
        name="mla_q_up",
    )(cq, w, g_nope, g_rope, rope_tab)


def _attn_body(q_ref, k_ref, v_ref, o_ref, s_ref):
    qi = pl.program_id(1)
    t = q_ref.shape[0]
    q = q_ref[...]

    def scores(j, slot):
        s_ref[slot] = _dot_nt(q, k_ref[pl.ds(pl.multiple_of(j * t, t), t), :])

    def update(j, slot, carry, masked):
        m, acc = carry
        s = s_ref[slot]
        if masked:
            row = lax.broadcasted_iota(jnp.int32, (t, t), 0)
            col = lax.broadcasted_iota(jnp.int32, (t, t), 1)
            s = jnp.where(col <= row, s, NEG_INF)
        m_new = jnp.maximum(m, jnp.max(s, axis=-1, keepdims=True))
        p = jnp.exp2(s - m_new)
        v = v_ref[pl.ds(pl.multiple_of(j * t, t), t), :]
        acc = jnp.exp2(m - m_new) * acc + _dot(p.astype(v.dtype), v)
        return m_new, acc

    def pair(i, carry):
        scores(2 * i + 1, 1)
        carry = update(2 * i, 0, carry, False)
        scores(2 * i + 2, 0)
        return update(2 * i + 1, 1, carry, False)

    def odd_tail(carry):
        scores(qi, 1)
        return update(qi, 1, update(qi - 1, 0, carry, False), True)

    def even_tail(carry):
        return update(qi, 0, carry, True)

    scores(0, 0)
    init = (jnp.full((t, 1), NEG_INF, F32), jnp.zeros((t, v_ref.shape[1]), F32))
    carry = lax.fori_loop(0, qi // 2, pair, init)
    _, acc = lax.cond(qi % 2 == 1, odd_tail, even_tail, carry)
    o_ref[...] = (acc[:, :HEAD_DIM] / acc[:, HEAD_DIM:HEAD_DIM + 1]).astype(o_ref.dtype)


def _attention(q_cat, k_cat, v, heads, t=1024):
    s = q_cat.shape[0]
    t = min(t, s)
    return pl.pallas_call(
        _attn_body,
        grid=(heads, s // t),
        in_specs=[pl.BlockSpec((t, 2 * HEAD_DIM), lambda h, i: (i, h)),
                  pl.BlockSpec((s, 2 * HEAD_DIM), lambda h, i: (0, h)),
                  pl.BlockSpec((s, 2 * HEAD_DIM), lambda h, i: (0, h))],
        out_specs=pl.BlockSpec((t, HEAD_DIM), lambda h, i: (i, h)),
        out_shape=jax.ShapeDtypeStruct((s, heads * HEAD_DIM), BF16),
        scratch_shapes=[pltpu.VMEM((2, t, t), F32)],
        compiler_params=_params("parallel", "parallel"),
        name="mla_attention",
    )(q_cat, k_cat, v)


def _router_body(h_ref, w_ref, b_ref, info_ref, cnt_ref, run_ref, *, experts):
    i = pl.program_id(0)

    @pl.when(i == 0)
    def _():
        run_ref[...] = jnp.zeros_like(run_ref)

    h = h_ref[...].astype(BF16)
    tm = h.shape[0]
    w = w_ref[...]
    w_hi = w.astype(BF16)
    rem = w - w_hi.astype(F32)
    w_mid = rem.astype(BF16)
    w_lo = (rem - w_mid.astype(F32)).astype(BF16)
    logits = _dot(h, w_hi) + _dot(h, w_mid) + _dot(h, w_lo) + b_ref[...]
    lane = lax.broadcasted_iota(jnp.int32, (tm, LANES), 1)
    logits = jnp.where(lane < experts, logits, NEG_INF)
    e = jnp.exp(logits - jnp.max(logits, axis=-1, keepdims=True))
    probs = e / jnp.sum(e, axis=-1, keepdims=True)
    probs = jnp.where(lane < experts, probs, -1.0)
    p1 = jnp.max(probs, axis=-1, keepdims=True)
    i1 = jnp.min(jnp.where(probs == p1, lane, LANES), axis=-1, keepdims=True)
    rest = jnp.where(lane == i1, -1.0, probs)
    p2 = jnp.max(rest, axis=-1, keepdims=True)
    i2 = jnp.min(jnp.where(rest == p2, lane, LANES), axis=-1, keepdims=True)
    total = p1 + p2
    hot1 = (lane == i1).astype(F32)
    hot2 = (lane == i2).astype(F32)
    hot = hot1 + hot2
    r = lax.broadcasted_iota(jnp.int32, (tm, tm), 0)
    c = lax.broadcasted_iota(jnp.int32, (tm, tm), 1)
    before = (c < r).astype(BF16)
    rank = _dot(before, hot.astype(BF16)) + run_ref[...]
    rank1 = jnp.sum(rank * hot1, axis=-1, keepdims=True)
    rank2 = jnp.sum(rank * hot2, axis=-1, keepdims=True)
    vals = (i1.astype(F32), i2.astype(F32), p1 / total, p2 / total, rank1, rank2)
    info = jnp.zeros((tm, LANES), F32)
    for idx, val in enumerate(vals):
        info = jnp.where(lane == idx, val, info)
    info_ref[...] = info
    run_ref[...] += jnp.sum(hot, axis=0, keepdims=True)
    cnt_ref[...] = run_ref[...]


def _router(h, w_pad, b_pad, experts, tm=512):
    s, d = h.shape
    tm = min(tm, s)
    return pl.pallas_call(
        functools.partial(_router_body, experts=experts),
        grid=(s // tm,),
        in_specs=[pl.BlockSpec((tm, d), lambda i: (i, 0)),
                  pl.BlockSpec((d, LANES), lambda i: (0, 0)),
                  pl.BlockSpec((1, LANES), lambda i: (0, 0))],
        out_specs=[pl.BlockSpec((tm, LANES), lambda i: (i, 0)),
                   pl.BlockSpec((1, LANES), lambda i: (0, 0))],
        out_shape=[jax.ShapeDtypeStruct((s, LANES), F32), jax.ShapeDtypeStruct((1, LANES), F32)],
        scratch_shapes=[pltpu.VMEM((1, LANES), F32)],
        compiler_params=_params("arbitrary"),
    )(h, w_pad, b_pad)


def _row_copy(src_ref, src_row, dst_ref, dst_row, sem):
    return pltpu.make_async_copy(src_ref.at[pl.ds(src_row, 1), :], dst_ref.at[pl.ds(dst_row, 1), :], sem)


def _dispatch_body(src_ref, h_ref, o_ref, buf_ref, sem):
    rows = o_ref.shape[0]
    base = pl.program_id(0) * rows

    @pl.when(pl.program_id(0) == 0)
    def _():
        buf_ref[...] = jnp.zeros_like(buf_ref)

    def start(n, carry):
        row = src_ref[base + n]

        @pl.when(row >= 0)
        def _():
            _row_copy(h_ref, row, buf_ref, n, sem).start()

        return carry

    def wait(n, carry):
        row = src_ref[base + n]

        @pl.when(row >= 0)
        def _():
            _row_copy(h_ref, row, buf_ref, n, sem).wait()

        return carry

    lax.fori_loop(0, rows, start, 0, unroll=8)
    lax.fori_loop(0, rows, wait, 0, unroll=8)
    o_ref[...] = buf_ref[...].astype(o_ref.dtype)


def _dispatch(src, h, rows=512):
    d = h.shape[1]
    padded_rows = src.shape[0]
    return pl.pallas_call(
        _dispatch_body,
        grid_spec=pltpu.PrefetchScalarGridSpec(
            num_scalar_prefetch=1,
            grid=(padded_rows // rows,),
            in_specs=[pl.BlockSpec(memory_space=pl.ANY)],
            out_specs=pl.BlockSpec((rows, d), lambda i, src: (i, 0)),
            scratch_shapes=[pltpu.VMEM((rows, d), h.dtype), pltpu.SemaphoreType.DMA(())]),
        out_shape=jax.ShapeDtypeStruct((padded_rows, d), BF16),
        compiler_params=_params("arbitrary"),
        name="moe_dispatch",
    )(src, h)


MOE_STEP_SPARE, MOE_STEP_REUSE, MOE_STEP_NEW_WEIGHTS = 0, 1, 2


def _moe_up_body(tile_ref, col_ref, exp_ref, wcol_ref, kind_ref, a_ref, wg_ref, wu_ref, o_ref, wg16_ref, wu16_ref):
    del tile_ref, col_ref, exp_ref, wcol_ref
    kind = kind_ref[pl.program_id(0)]

    @pl.when(kind == MOE_STEP_NEW_WEIGHTS)
    def _():
        wg16_ref[...] = wg_ref[...].astype(BF16)
        wu16_ref[...] = wu_ref[...].astype(BF16)

    @pl.when(kind != MOE_STEP_SPARE)
    def _():
        a = a_ref[...]
        g = _dot(a, wg16_ref[...])
        u = _dot(a, wu16_ref[...])
        o_ref[...] = (_silu(g) * u).astype(o_ref.dtype)

    @pl.when(kind == MOE_STEP_SPARE)
    def _():
        o_ref[...] = jnp.zeros_like(o_ref)


def _moe_down_body(tile_ref, col_ref, exp_ref, wcol_ref, kind_ref, a_ref, w_ref, o_ref, w16_ref):
    del tile_ref, col_ref, exp_ref, wcol_ref
    kind = kind_ref[pl.program_id(0)]

    @pl.when(kind == MOE_STEP_NEW_WEIGHTS)
    def _():
        w16_ref[...] = w_ref[...].astype(BF16)

    @pl.when(kind != MOE_STEP_SPARE)
    def _():
        o_ref[...] = _dot(a_ref[...], w16_ref[...])

    @pl.when(kind == MOE_STEP_SPARE)
    def _():
        o_ref[...] = jnp.zeros_like(o_ref)


def _moe_matmul(body, name, sched, a, weights, out_dtype, tm, tn):
    rows = a.shape[0]
    _, k, n = weights[0].shape
    wspec = pl.BlockSpec((None, k, tn), lambda s, tile, col, exp, wcol, valid: (exp[s], 0, wcol[s]))
    return pl.pallas_call(
        body,
        grid_spec=pltpu.PrefetchScalarGridSpec(
            num_scalar_prefetch=5,
            grid=(sched[0].shape[0],),
            in_specs=[pl.BlockSpec((tm, a.shape[1]), lambda s, tile, col, exp, wcol, valid: (tile[s], 0))]
            + [wspec] * len(weights),
            out_specs=pl.BlockSpec((tm, tn), lambda s, tile, col, exp, wcol, valid: (tile[s], col[s])),
            scratch_shapes=[pltpu.VMEM((k, tn), BF16)] * len(weights)),
        out_shape=jax.ShapeDtypeStruct((rows, n), out_dtype),
        compiler_params=_params("arbitrary"),
        name=name,
    )(*sched, a, *weights)


def _combine_body(slot_ref, y_ref, x_ref, info_ref, gate_ref, o_ref, buf_ref, sem, *, tokens):
    base = pl.program_id(0) * tokens

    def copy(n):
        return _row_copy(y_ref, slot_ref[base * MOE_TOP_K + n], buf_ref.at[n % MOE_TOP_K], n // MOE_TOP_K, sem)

    def start(n, carry):
        copy(n).start()
        return carry

    def wait(n, carry):
        copy(n).wait()
        return carry

    lax.fori_loop(0, tokens * MOE_TOP_K, start, 0, unroll=8)
    lax.fori_loop(0, tokens * MOE_TOP_K, wait, 0, unroll=8)
    info = info_ref[...]
    mix = info[:, 2:3] * buf_ref[0] + info[:, 3:4] * buf_ref[1]
    o_ref[...] = x_ref[...] + gate_ref[...] * mix


def _combine(slots, y, x, info, gate, tokens=256):
    s, d = x.shape
    tokens = min(tokens, s)
    return pl.pallas_call(
        functools.partial(_combine_body, tokens=tokens),
        grid_spec=pltpu.PrefetchScalarGridSpec(
            num_scalar_prefetch=1,
            grid=(s // tokens,),
            in_specs=[pl.BlockSpec(memory_space=pl.ANY),
                      pl.BlockSpec((tokens, d), lambda i, slot: (i, 0)),
                      pl.BlockSpec((tokens, LANES), lambda i, slot: (i, 0)),
                      pl.BlockSpec((1, d), lambda i, slot: (0, 0))],
            out_specs=pl.BlockSpec((tokens, d), lambda i, slot: (i, 0)),
            scratch_shapes=[pltpu.VMEM((MOE_TOP_K, tokens, d), F32), pltpu.SemaphoreType.DMA(())]),
        out_shape=jax.ShapeDtypeStruct((s, d), F32),
        compiler_params=_params("arbitrary"),
    )(slots, y, x, info, gate.reshape(1, d))


def _moe_schedule(info, counts, experts, tm, n_col):
    s = info.shape[0]
    max_tiles = (s * MOE_TOP_K) // tm + experts
    ids = info[:, 0:MOE_TOP_K].astype(jnp.int32)
    ranks = info[:, 4:4 + MOE_TOP_K].astype(jnp.int32)
    cnt = counts[0, :experts].astype(jnp.int32)
    tiles = (cnt + tm - 1) // tm
    tile_end = jnp.cumsum(tiles)
    tile_start = tile_end - tiles
    slots = ((tile_start * tm)[ids] + ranks).reshape(-1)
    token = jnp.arange(s * MOE_TOP_K, dtype=jnp.int32) // MOE_TOP_K
    src = jnp.full((max_tiles * tm,), -1, jnp.int32).at[slots].set(token)
    step_end = tile_end * n_col
    total = step_end[-1]
    steps = jnp.arange(max_tiles * n_col, dtype=jnp.int32)
    valid = steps < total
    st = jnp.minimum(steps, total - 1)
    exp = jnp.sum(st[:, None] >= step_end[None, :], axis=1)
    local = st - (step_end - tiles * n_col)[exp]
    wcol = local // tiles[exp]
    spare = steps - total
    col = jnp.where(valid, wcol, spare % n_col)
    tile = jnp.where(valid, tile_start[exp] + local % tiles[exp], tile_end[-1] + spare // n_col)
    first = (local % tiles[exp]) == 0
    kind = jnp.where(valid, jnp.where(first, MOE_STEP_NEW_WEIGHTS, MOE_STEP_REUSE), MOE_STEP_SPARE)
    sched = tuple(v.astype(jnp.int32) for v in (tile, col, exp, wcol, kind))
    return slots.astype(jnp.int32), src, sched


def _rope_tables(positions):
    inv_freq = ROPE_THETA ** (-jnp.arange(0, ROPE_DIM, 2, dtype=F32) / ROPE_DIM)
    ang = positions.astype(F32)[:, None] * inv_freq
    cos, sin = jnp.cos(ang), jnp.sin(ang)
    zero = jnp.zeros_like(cos)
    pad = jnp.zeros((cos.shape[0], LANES - ROPE_DIM), F32)
    key_tabs = (jnp.concatenate([cos, cos, pad], axis=1),
                jnp.concatenate([-sin, zero, pad], axis=1),
                jnp.concatenate([zero, sin, pad], axis=1))
    return key_tabs, jnp.concatenate([cos, cos, -sin, sin], axis=1)


def _pad_lanes(v):
    return jnp.pad(v, ((0, 0), (0, LANES - v.shape[1])))


def kernel(x, c, positions, ada_w, ada_b, norm_mix, norm_ffn, gdn_w_in, gdn_conv, gdn_a_log, gdn_dt_bias, gdn_out_norm, gdn_w_out, kv_norm, w_dkv, kv_latent_norm, w_ukv, k_nope_norm, k_rope_norm, mla_w_dq, mla_q_latent_norm, mla_w_uq, mla_q_nope_norm, mla_q_rope_norm, mla_w_out, ffn_w_gate, ffn_w_up, ffn_w_down, router_w, router_b, moe_w_gate, moe_w_up, moe_w_down):
    assert x.shape[0] == 1 and ada_w.shape[0] == 2
    _, s, d = x.shape
    heads = d // HEAD_DIM
    width = heads * HEAD_DIM
    x0 = x[0]

    mod = _ada_mod(c, ada_w, ada_b).reshape(2, ADA_CHUNKS, d)
    rope_tabs, q_rope_tab = _rope_tables(positions[0])

    sh_m, sc_m, g_m, sh_f, sc_f, g_f = (mod[0, n] for n in range(ADA_CHUNKS))
    h = _norm_mod(x0, norm_mix[0], sc_m, sh_m, name="norm_mix0")
    w_in = gdn_w_in[0].astype(BF16)
    proj = _matmul(h, w_in, name="gdn_in_proj", out_dtype=BF16, n_cols=4 * width)
    w_ba = w_in[:, 4 * width:]
    b_raw = _matmul(h, _pad_lanes(w_ba[:, :heads]), name="gdn_beta_proj", out_dtype=F32)
    a_raw = _matmul(h, _pad_lanes(w_ba[:, heads:]), name="gdn_decay_proj", out_dtype=F32)
    qkv = _gdn_conv(proj, gdn_conv[0], width)
    gcb, bb, gct = _gdn_gate(b_raw, a_raw, _pad_lanes(gdn_a_log[0][None]), _pad_lanes(gdn_dt_bias[0][None]), width)
    o = _gdn_chunk(qkv, proj, gcb, bb, gct[:heads], gdn_out_norm[0], heads)
    x1 = _matmul(o, gdn_w_out[0].astype(BF16), name="gdn_out_proj", out_dtype=F32, res=x0, gate=g_m)

    h = _norm_mod(x1, norm_ffn[0], sc_f, sh_f, name="norm_ffn0")
    hid = _swiglu_up(h, ffn_w_gate[0].astype(BF16), ffn_w_up[0].astype(BF16))
    ffn_dim = hid.shape[1]
    x2 = _matmul_ktiled_res(hid, ffn_w_down[0].astype(BF16), x1, g_f, tk=ffn_dim // 2)

    lat = kv_latent_norm.shape[0]
    zeros = jnp.zeros((d,), F32)
    h_kv = _norm_mod(x2, kv_norm, zeros, zeros, name="norm_kv", modulate=False)
    w_dkv16 = w_dkv.astype(BF16)
    ckv, k_rope = _kv_down(h_kv, w_dkv16[:, :lat], _pad_lanes(w_dkv16[:, lat:]), kv_latent_norm[None],
                           _pad_lanes(k_rope_norm[None]), rope_tabs)
    k_cat, v = _kv_up(ckv, w_ukv.astype(BF16), k_rope, k_nope_norm[None])

    sh_m, sc_m, g_m, sh_f, sc_f, g_f = (mod[1, n] for n in range(ADA_CHUNKS))
    h = _norm_mod(x2, norm_mix[1], sc_m, sh_m, name="norm_mix1")
    cq = _q_down(h, mla_w_dq[0].astype(BF16), mla_q_latent_norm[0])
    q_lora = cq.shape[1]
    w_uq = mla_w_uq[0].astype(BF16).reshape(q_lora, heads, HEAD_DIM + ROPE_DIM)
    half = ROPE_DIM // 2
    swap = lambda t: jnp.concatenate([t[..., half:], t[..., :half]], axis=-1)
    w_rope = w_uq[..., HEAD_DIM:]
    w_uq = jnp.concatenate([w_uq, swap(w_rope)], axis=-1).reshape(q_lora, heads * 2 * HEAD_DIM)
    g_rope = mla_q_rope_norm[0][None]
    q_cat = _q_up(cq, w_uq, mla_q_nope_norm[0][None], jnp.concatenate([g_rope, swap(g_rope)], axis=1),
                  q_rope_tab, (HEAD_DIM + ROPE_DIM) ** -0.5 * math.log2(math.e))
    o = _attention(q_cat, k_cat, v, heads)
    x3 = _matmul(o, mla_w_out[0].astype(BF16), name="mla_out_proj", out_dtype=F32, res=x2, gate=g_m)

    experts = router_w.shape[-1]
    moe_tm, moe_tn = 512, 512
    assert moe_w_gate.shape[-1] == d
    h = _norm_mod(x3, norm_ffn[1], sc_f, sh_f, name="norm_ffn1", out_dtype=F32)
    info, counts = _router(h, _pad_lanes(router_w[0]), _pad_lanes(router_b[0][None]), experts)
    slots, src, sched = _moe_schedule(info, counts, experts, moe_tm, d // moe_tn)
    sorted_h = _dispatch(src, h, moe_tm)
    hid = _moe_matmul(_moe_up_body, "moe_up", sched, sorted_h, (moe_w_gate[0], moe_w_up[0]), BF16, moe_tm, moe_tn)
    y = _moe_matmul(_moe_down_body, "moe_down", sched, hid, (moe_w_down[0],), F32, moe_tm, moe_tn)
    out = _combine(slots, y, x3, info, g_f)
    return out[None]
```

```python
import functools
import math

import jax
import jax.numpy as jnp
from jax import lax
from jax.experimental import pallas as pl
from jax.experimental.pallas import tpu as pltpu

F32 = jnp.float32
BF16 = jnp.bfloat16

RMS_EPS = 1e-6
L2_EPS = 1e-6
NEG_INF = -1e30
ROPE_THETA = 10000.0

LANES = 128
HEAD_DIM = 128
ROPE_DIM = 64
GDN_CHUNK = 64
GDN_CONV = 4
MOE_TOP_K = 2
ADA_CHUNKS = 6
VMEM_LIMIT_BYTES = 56 * 1024 * 1024


def _params(*semantics):
    return pltpu.CompilerParams(dimension_semantics=semantics, vmem_limit_bytes=VMEM_LIMIT_BYTES)


def _dot(a, b):
    return jnp.dot(a, b, preferred_element_type=F32)


def _dot_nt(a, b):
    return lax.dot_general(a, b, (((1,), (1,)), ((), ())), preferred_element_type=F32)


def _dot_tn(a, b):
    return lax.dot_general(a, b, (((0,), (0,)), ((), ())), preferred_element_type=F32)


def _silu(x):
    return x * jax.nn.sigmoid(x)


def _split3(x):
    hi = x.astype(BF16)
    rem = x - hi.astype(F32)
    mid = rem.astype(BF16)
    return hi, mid, (rem - mid.astype(F32)).astype(BF16)


def _rms(x, gain, n):
    ms = jnp.sum(x * x, axis=-1, keepdims=True) * (1.0 / n)
    return x * lax.rsqrt(ms + RMS_EPS) * gain


def _ada_body(c_ref, w_ref, b_ref, o_ref):
    cs = _silu(c_ref[...])
    for j in range(o_ref.shape[-1] // LANES):
        sl = slice(j * LANES, (j + 1) * LANES)
        s = jnp.sum(w_ref[0, :, sl] * cs, axis=0, keepdims=True)
        o_ref[0, :, sl] = s + b_ref[0, :, sl]


def _ada_mod(c, ada_w, ada_b, tn=1024):
    depth, d, n = ada_w.shape
    c_b = jnp.broadcast_to(c.reshape(d, 1), (d, LANES))
    return pl.pallas_call(
        _ada_body,
        grid=(depth, n // tn),
        in_specs=[pl.BlockSpec((d, LANES), lambda l, j: (0, 0)),
                  pl.BlockSpec((1, d, tn), lambda l, j: (l, 0, j)),
                  pl.BlockSpec((1, 1, tn), lambda l, j: (l, 0, j))],
        out_specs=pl.BlockSpec((1, 1, tn), lambda l, j: (l, 0, j)),
        out_shape=jax.ShapeDtypeStruct((depth, 1, n), F32),
        compiler_params=_params("parallel", "parallel"),
    )(c_b, ada_w, ada_b.reshape(depth, 1, n))


def _norm_body(x_ref, g_ref, sc_ref, sh_ref, o_ref, *, modulate):
    x = x_ref[...]
    y = _rms(x, g_ref[...], x.shape[-1])
    if modulate:
        y = y * (1.0 + sc_ref[...]) + sh_ref[...]
    o_ref[...] = y.astype(o_ref.dtype)


def _norm_mod(x, gain, scale, shift, *, name, modulate=True, out_dtype=BF16, tr=256):
    s, d = x.shape
    tr = min(tr, s)
    vec = pl.BlockSpec((1, d), lambda i: (0, 0))
    return pl.pallas_call(
        functools.partial(_norm_body, modulate=modulate),
        grid=(s // tr,),
        in_specs=[pl.BlockSpec((tr, d), lambda i: (i, 0)), vec, vec, vec],
        out_specs=pl.BlockSpec((tr, d), lambda i: (i, 0)),
        out_shape=jax.ShapeDtypeStruct((s, d), out_dtype),
        compiler_params=_params("parallel"),
        name=name,
    )(x, gain.reshape(1, d), scale.reshape(1, d), shift.reshape(1, d))


def _mm_body(a_ref, w_ref, o_ref):
    o_ref[...] = _dot(a_ref[...], w_ref[...].astype(BF16)).astype(o_ref.dtype)


def _mm_res_body(a_ref, w_ref, r_ref, g_ref, o_ref):
    o_ref[...] = r_ref[...] + g_ref[...] * _dot(a_ref[...], w_ref[...].astype(BF16))


def _matmul(a, w, *, name, out_dtype, tm=1024, tn=512, res=None, gate=None, n_cols=None):
    m, k = a.shape
    n = w.shape[1] if n_cols is None else n_cols
    tm, tn = min(tm, m), min(tn, n)
    in_specs = [pl.BlockSpec((tm, k), lambda i, j: (i, 0)),
                pl.BlockSpec((k, tn), lambda i, j: (0, j))]
    args = [a, w]
    body = _mm_body
    if res is not None:
        in_specs += [pl.BlockSpec((tm, tn), lambda i, j: (i, j)),
                     pl.BlockSpec((1, tn), lambda i, j: (0, j))]
        args += [res, gate.reshape(1, n)]
        body = _mm_res_body
    return pl.pallas_call(
        body,
        grid=(m // tm, n // tn),
        in_specs=in_specs,
        out_specs=pl.BlockSpec((tm, tn), lambda i, j: (i, j)),
        out_shape=jax.ShapeDtypeStruct((m, n), out_dtype),
        compiler_params=_params("parallel", "parallel"),
        name=name,
    )(*args)


def _mm_ktiled_res_body(a_ref, w_ref, r_ref, g_ref, o_ref, acc_ref):
    kk = pl.program_id(2)

    @pl.when(kk == 0)
    def _():
        acc_ref[...] = jnp.zeros_like(acc_ref)

    acc_ref[...] += _dot(a_ref[...], w_ref[...].astype(BF16))

    @pl.when(kk == pl.num_programs(2) - 1)
    def _():
        o_ref[...] = r_ref[...] + g_ref[...] * acc_ref[...]


def _matmul_ktiled_res(a, w, res, gate, *, tm=512, tn=1024, tk):
    m, k = a.shape
    n = w.shape[1]
    tm, tn = min(tm, m), min(tn, n)
    return pl.pallas_call(
        _mm_ktiled_res_body,
        grid=(m // tm, n // tn, k // tk),
        in_specs=[pl.BlockSpec((tm, tk), lambda i, j, kk: (i, kk)),
                  pl.BlockSpec((tk, tn), lambda i, j, kk: (kk, j)),
                  pl.BlockSpec((tm, tn), lambda i, j, kk: (i, j)),
                  pl.BlockSpec((1, tn), lambda i, j, kk: (0, j))],
        out_specs=pl.BlockSpec((tm, tn), lambda i, j, kk: (i, j)),
        out_shape=jax.ShapeDtypeStruct((m, n), F32),
        scratch_shapes=[pltpu.VMEM((tm, tn), F32)],
        compiler_params=_params("parallel", "parallel", "arbitrary"),
    )(a, w, res, gate.reshape(1, n))


def _swiglu_up_body(a_ref, wg_ref, wu_ref, o_ref):
    a = a_ref[...]
    g = _dot(a, wg_ref[...].astype(BF16))
    u = _dot(a, wu_ref[...].astype(BF16))
    o_ref[...] = (_silu(g) * u).astype(o_ref.dtype)


def _swiglu_up(a, wg, wu, *, tm=1024, tn=256):
    m, k = a.shape
    n = wg.shape[1]
    tm = min(tm, m)
    wspec = pl.BlockSpec((k, tn), lambda i, j: (0, j))
    return pl.pallas_call(
        _swiglu_up_body,
        grid=(m // tm, n // tn),
        in_specs=[pl.BlockSpec((tm, k), lambda i, j: (i, 0)), wspec, wspec],
        out_specs=pl.BlockSpec((tm, tn), lambda i, j: (i, j)),
        out_shape=jax.ShapeDtypeStruct((m, n), BF16),
        compiler_params=_params("parallel", "parallel"),
    )(a, wg, wu)


def _gdn_conv_body(prev_ref, x_ref, w_ref, o_ref):
    i = pl.program_id(0)
    which = pl.program_id(1)
    tr = x_ref.shape[0]
    x16 = x_ref[...]
    prev = prev_ref[...]
    rows2 = jnp.concatenate([jnp.where(i > 0, prev, jnp.zeros_like(prev)), x16], axis=0)
    r = lax.broadcasted_iota(jnp.int32, (tr, 2 * tr), 0)
    c = lax.broadcasted_iota(jnp.int32, (tr, 2 * tr), 1)
    taps = [_dot((c == r + tr - d).astype(BF16), rows2) for d in range(GDN_CONV - 1, 0, -1)]
    for h in range(x_ref.shape[1] // HEAD_DIM):
        sl = slice(h * HEAD_DIM, (h + 1) * HEAD_DIM)
        y = taps[0][:, sl] * w_ref[0:1, sl]
        for j in range(1, GDN_CONV - 1):
            y = y + taps[j][:, sl] * w_ref[j:j + 1, sl]
        y = y + x16[:, sl].astype(F32) * w_ref[GDN_CONV - 1:GDN_CONV, sl]
        y = _silu(y)
        ss = jnp.sum(y * y, axis=-1, keepdims=True)
        yn = y * lax.rsqrt(ss + L2_EPS)
        o_ref[:, sl] = jnp.where(which < 2, yn, y).astype(o_ref.dtype)


def _gdn_conv(proj, conv_w, width, tr=128):
    s = proj.shape[0]
    tr = min(tr, s)
    return pl.pallas_call(
        _gdn_conv_body,
        grid=(s // tr, 3),
        in_specs=[pl.BlockSpec((tr, width), lambda i, c: (jnp.maximum(i - 1, 0), c)),
                  pl.BlockSpec((tr, width), lambda i, c: (i, c)),
                  pl.BlockSpec((GDN_CONV, width), lambda i, c: (0, c))],
        out_specs=pl.BlockSpec((tr, width), lambda i, c: (i, c)),
        out_shape=jax.ShapeDtypeStruct((s, 3 * width), BF16),
        compiler_params=_params("parallel", "parallel"),
        name="gdn_conv",
    )(proj, proj, conv_w)


def _gdn_gate_body(b_ref, a_ref, alog_ref, dtb_ref, gcb_ref, bb_ref, gct_ref):
    tr = b_ref.shape[0]
    beta = jax.nn.sigmoid(b_ref[...])
    x = a_ref[...] + dtb_ref[...]
    softplus = jnp.maximum(x, 0.0) + jnp.log1p(jnp.exp(-jnp.abs(x)))
    g = -jnp.exp(alog_ref[...]) * softplus
    r = lax.broadcasted_iota(jnp.int32, (tr, tr), 0)
    c = lax.broadcasted_iota(jnp.int32, (tr, tr), 1)
    shift = int(math.log2(GDN_CHUNK))
    tri = ((r >> shift == c >> shift) & (c <= r)).astype(BF16)
    gc = sum(_dot(tri, part) for part in _split3(g))
    width = gcb_ref.shape[1]
    er = lax.broadcasted_iota(jnp.int32, (LANES, width), 0)
    ec = lax.broadcasted_iota(jnp.int32, (LANES, width), 1)
    expand = (ec >> int(math.log2(HEAD_DIM)) == er).astype(BF16)
    gc_parts = _split3(gc)
    gcb_ref[...] = sum(_dot(part, expand) for part in gc_parts)
    bb_ref[...] = sum(_dot(part, expand) for part in _split3(beta))
    ir = lax.broadcasted_iota(jnp.int32, (LANES, LANES), 0)
    ic = lax.broadcasted_iota(jnp.int32, (LANES, LANES), 1)
    eye = (ir == ic).astype(BF16)
    gct_ref[...] = sum(_dot_nt(eye, part) for part in gc_parts)


def _gdn_gate(b_raw, a_raw, a_log, dt_bias, width, tr=512):
    s = b_raw.shape[0]
    tr = min(tr, s)
    row = pl.BlockSpec((tr, LANES), lambda i: (i, 0))
    vec = pl.BlockSpec((1, LANES), lambda i: (0, 0))
    wide = pl.BlockSpec((tr, width), lambda i: (i, 0))
    return pl.pallas_call(
        _gdn_gate_body,
        grid=(s // tr,),
        in_specs=[row, row, vec, vec],
        out_specs=[wide, wide, pl.BlockSpec((LANES, tr), lambda i: (0, i))],
        out_shape=[jax.ShapeDtypeStruct((s, width), F32), jax.ShapeDtypeStruct((s, width), F32),
                   jax.ShapeDtypeStruct((LANES, s), F32)],
        compiler_params=_params("parallel"),
    )(b_raw, a_raw, a_log, dt_bias)


def _gdn_chunk_body(q_ref, k_ref, v_ref, gcb_ref, bb_ref, gct_ref, z_ref, gain_ref, o_ref, state_ref):
    @pl.when(pl.program_id(1) == 0)
    def _():
        state_ref[...] = jnp.zeros_like(state_ref)

    cs = GDN_CHUNK
    n_heads = state_ref.shape[0]
    n_chunks = q_ref.shape[0] // cs
    probs = [(j, n) for j in range(n_heads) for n in range(n_chunks)]

    def tile(ref, j, n):
        return ref[n * cs:(n + 1) * cs, j * HEAD_DIM:(j + 1) * HEAD_DIM]

    r = lax.broadcasted_iota(jnp.int32, (cs, cs), 0)
    c = lax.broadcasted_iota(jnp.int32, (cs, cs), 1)
    causal = c <= r
    strict = c < r
    eye = (r == c).astype(F32)

    k = [tile(k_ref, j, n).astype(F32) for j, n in probs]
    k16 = [x.astype(BF16) for x in k]
    gc = [tile(gcb_ref, j, n) for j, n in probs]
    beta = [tile(bb_ref, j, n) for j, n in probs]
    kb = [x * b for x, b in zip(k, beta)]
    decay = []
    for (j, n), g in zip(probs, gc):
        gdiff = g[:, :cs] - gct_ref[j, :, n * cs:(n + 1) * cs]
        decay.append(jnp.where(causal, jnp.exp(jnp.where(causal, gdiff, 0.0)), 0.0))
    q = [tile(q_ref, j, n).astype(F32) * (HEAD_DIM ** -0.5) for j, n in probs]
    kq = [_dot_nt(jnp.concatenate([x.astype(BF16), y.astype(BF16)], axis=0), z) for x, y, z in zip(kb, q, k16)]
    p = [jnp.where(strict, -(x[:cs] * d), 0.0) for x, d in zip(kq, decay)]
    inv = [eye + x for x in p]
    for _ in range(int(math.log2(cs)) - 1):
        p16 = [x.astype(BF16) for x in p]
        p = [_dot(x, x) for x in p16]
        inv = [x + _dot(x.astype(BF16), y.astype(BF16)) for x, y in zip(inv, p)]
    t16 = [x.astype(BF16) for x in inv]
    uw = [_dot(t, jnp.concatenate([(tile(v_ref, j, n).astype(F32) * b).astype(BF16),
                                   (x * jnp.exp(g)).astype(BF16)], axis=1))
          for t, (j, n), b, x, g in zip(t16, probs, beta, kb, gc)]
    qk = [(x[cs:] * d).astype(BF16) for x, d in zip(kq, decay)]
    wq = [jnp.concatenate([x[:, HEAD_DIM:].astype(BF16), (y * jnp.exp(g)).astype(BF16)], axis=0)
          for x, y, g in zip(uw, q, gc)]
    g_last = [g[cs - 1:cs, :] for g in gc]
    k_end = [(x * jnp.exp(gl - g)).astype(BF16) for x, gl, g in zip(k, g_last, gc)]

    state = [state_ref[j] for j in range(n_heads)]
    for n in range(n_chunks):
        idx = [j * n_chunks + n for j in range(n_heads)]
        ws = [_dot(wq[i], s.astype(BF16)) for i, s in zip(idx, state)]
        v16 = [(uw[i][:, :HEAD_DIM] - x[:cs]).astype(BF16) for i, x in zip(idx, ws)]
        o = [x[cs:] + _dot(qk[i], v) for i, x, v in zip(idx, ws, v16)]
        state = [s * jnp.exp(g_last[i]) + _dot_tn(k_end[i], v) for i, s, v in zip(idx, state, v16)]
        for j in range(n_heads):
            z = tile(z_ref, j, n).astype(F32)
            o_ref[n * cs:(n + 1) * cs, j * HEAD_DIM:(j + 1) * HEAD_DIM] = (
                _rms(o[j], gain_ref[...], HEAD_DIM) * _silu(z)).astype(o_ref.dtype)
    for j in range(n_heads):
        state_ref[j] = state[j]


def _gdn_chunk(qkv, proj, gcb, bb, gct, out_norm, heads, rows=256, group=8):
    s = qkv.shape[0]
    rows = min(rows, s)
    groups = heads // group
    blk = lambda off: pl.BlockSpec((rows, group * HEAD_DIM), lambda h, r: (r, off + h))
    return pl.pallas_call(
        _gdn_chunk_body,
        grid=(groups, s // rows),
        in_specs=[blk(0), blk(groups), blk(2 * groups), blk(0), blk(0),
                  pl.BlockSpec((group, 1, rows), lambda h, r: (h, 0, r)),
                  blk(3 * groups),
                  pl.BlockSpec((1, HEAD_DIM), lambda h, r: (0, 0))],
        out_specs=blk(0),
        out_shape=jax.ShapeDtypeStruct((s, heads * HEAD_DIM), BF16),
        scratch_shapes=[pltpu.VMEM((group, HEAD_DIM, HEAD_DIM), F32)],
        compiler_params=_params("parallel", "arbitrary"),
        name="gdn_chunk",
    )(qkv, qkv, qkv, gcb, bb, gct.reshape(gct.shape[0], 1, s), proj, out_norm.reshape(1, HEAD_DIM))


def _rope(x, cos_ref, sin_a_ref, sin_b_ref):
    half = ROPE_DIM // 2
    return (x * cos_ref[...] + pltpu.roll(x, LANES - half, 1) * sin_a_ref[...]
            + pltpu.roll(x, half, 1) * sin_b_ref[...])


def _kv_down_body(a_ref, w1_ref, w2_ref, g1_ref, g2_ref, cos_ref, sa_ref, sb_ref, ckv_ref, kr_ref):
    a = a_ref[...]
    ckv = _dot(a, w1_ref[...].astype(BF16))
    ckv_ref[...] = _rms(ckv, g1_ref[...], ckv.shape[-1]).astype(ckv_ref.dtype)
    kr = _rope(_rms(_dot(a, w2_ref[...].astype(BF16)), g2_ref[...], ROPE_DIM), cos_ref, sa_ref, sb_ref)
    kr_ref[...] = (kr + pltpu.roll(kr, ROPE_DIM, 1)).astype(kr_ref.dtype)


def _kv_down(a, w_c, w_r, g_c, g_r, rope_tabs, tm=512):
    m, k = a.shape
    lat = w_c.shape[1]
    tm = min(tm, m)
    full = lambda shp: pl.BlockSpec(shp, lambda i: (0, 0))
    row = lambda w: pl.BlockSpec((tm, w), lambda i: (i, 0))
    return pl.pallas_call(
        _kv_down_body,
        grid=(m // tm,),
        in_specs=[row(k), full((k, lat)), full((k, LANES)), full((1, lat)), full((1, LANES)),
                  row(LANES), row(LANES), row(LANES)],
        out_specs=[row(lat), row(LANES)],
        out_shape=[jax.ShapeDtypeStruct((m, lat), BF16), jax.ShapeDtypeStruct((m, LANES), BF16)],
        compiler_params=_params("parallel"),
    )(a, w_c, w_r, g_c, g_r, *rope_tabs)


def _kv_up_body(a_ref, w_ref, kr_ref, g_ref, k_ref, v_ref):
    acc = _dot(a_ref[...], w_ref[...].astype(BF16))
    ones = jnp.ones((acc.shape[0], HEAD_DIM), v_ref.dtype)
    for h in range(acc.shape[1] // (2 * HEAD_DIM)):
        base = 2 * HEAD_DIM * h
        kn = acc[:, base:base + HEAD_DIM]
        k_ref[:, base:base + HEAD_DIM] = _rms(kn, g_ref[...], HEAD_DIM).astype(k_ref.dtype)
        k_ref[:, base + HEAD_DIM:base + 2 * HEAD_DIM] = kr_ref[...]
        v_ref[:, base:base + HEAD_DIM] = acc[:, base + HEAD_DIM:base + 2 * HEAD_DIM].astype(v_ref.dtype)
        v_ref[:, base + HEAD_DIM:base + 2 * HEAD_DIM] = ones


def _kv_up(ckv, w_ukv, kr, g_nope, tm=1024, hps=4):
    m, lat = ckv.shape
    n = w_ukv.shape[1]
    tm = min(tm, m)
    tn = hps * 2 * HEAD_DIM
    return pl.pallas_call(
        _kv_up_body,
        grid=(m // tm, n // tn),
        in_specs=[pl.BlockSpec((tm, lat), lambda i, j: (i, 0)),
                  pl.BlockSpec((lat, tn), lambda i, j: (0, j)),
                  pl.BlockSpec((tm, LANES), lambda i, j: (i, 0)),
                  pl.BlockSpec((1, HEAD_DIM), lambda i, j: (0, 0))],
        out_specs=[pl.BlockSpec((tm, tn), lambda i, j: (i, j)),
                   pl.BlockSpec((tm, tn), lambda i, j: (i, j))],
        out_shape=[jax.ShapeDtypeStruct((m, n), BF16), jax.ShapeDtypeStruct((m, n), BF16)],
        compiler_params=_params("parallel", "parallel"),
        name="mla_kv_up",
    )(ckv, w_ukv, kr, g_nope)


def _q_down_body(a_ref, w_ref, g_ref, o_ref):
    cq = _dot(a_ref[...], w_ref[...].astype(BF16))
    o_ref[...] = _rms(cq, g_ref[...], cq.shape[-1]).astype(o_ref.dtype)


def _q_down(a, w, gain, tm=512):
    m, k = a.shape
    n = w.shape[1]
    tm = min(tm, m)
    return pl.pallas_call(
        _q_down_body,
        grid=(m // tm,),
        in_specs=[pl.BlockSpec((tm, k), lambda i: (i, 0)),
                  pl.BlockSpec((k, n), lambda i: (0, 0)),
                  pl.BlockSpec((1, n), lambda i: (0, 0))],
        out_specs=pl.BlockSpec((tm, n), lambda i: (i, 0)),
        out_shape=jax.ShapeDtypeStruct((m, n), BF16),
        compiler_params=_params("parallel"),
    )(a, w, gain.reshape(1, n))


def _q_up_body(a_ref, w_ref, gn_ref, gr_ref, tab_ref, o_ref, *, scale):
    acc = _dot(a_ref[...], w_ref[...].astype(BF16))
    first = (lax.broadcasted_iota(jnp.int32, (1, LANES), 1) < ROPE_DIM).astype(F32)
    rope_gain = gr_ref[...] * tab_ref[...] * scale
    for h in range(acc.shape[1] // (2 * HEAD_DIM)):
        base = 2 * HEAD_DIM * h
        qn = _rms(acc[:, base:base + HEAD_DIM], gn_ref[...], HEAD_DIM)
        o_ref[:, base:base + HEAD_DIM] = (qn * scale).astype(o_ref.dtype)
        x = acc[:, base + HEAD_DIM:base + 2 * HEAD_DIM]
        ms = jnp.sum(x * x * first, axis=-1, keepdims=True) * (1.0 / ROPE_DIM)
        o_ref[:, base + HEAD_DIM:base + 2 * HEAD_DIM] = (x * lax.rsqrt(ms + RMS_EPS) * rope_gain).astype(o_ref.dtype)


def _q_up(cq, w, g_nope, g_rope, rope_tab, scale, tm=1024, hps=4):
    m, lat = cq.shape
    n = w.shape[1]
    tm = min(tm, m)
    tn = hps * 2 * HEAD_DIM
    vec = pl.BlockSpec((1, LANES), lambda i, j: (0, 0))
    return pl.pallas_call(
        functools.partial(_q_up_body, scale=scale),
        grid=(m // tm, n // tn),
        in_specs=[pl.BlockSpec((tm, lat), lambda i, j: (i, 0)),
                  pl.BlockSpec((lat, tn), lambda i, j: (0, j)),
                  vec, vec, pl.BlockSpec((tm, LANES), lambda i, j: (i, 0))],
        out_specs=pl.BlockSpec((tm, tn), lambda i, j: (i, j)),
        out_shape=jax.ShapeDtypeStruct((m, n), BF16),
        compiler_params=_params("parallel", "parallel"),
        name="mla_q_up",
    )(cq, w, g_nope, g_rope, rope_tab)


def _attn_body(q_ref, k_ref, v_ref, o_ref, s_ref):
    qi = pl.program_id(1)
    t = q_ref.shape[0]
    q = q_ref[...]

    def scores(j, slot):
        s_ref[slot] = _dot_nt(q, k_ref[pl.ds(pl.multiple_of(j * t, t), t), :])

    def update(j, slot, carry, masked):
        m, acc = carry
        s = s_ref[slot]
        if masked:
            row = lax.broadcasted_iota(jnp.int32, (t, t), 0)
            col = lax.broadcasted_iota(jnp.int32, (t, t), 1)
            s = jnp.where(col <= row, s, NEG_INF)
        m_new = jnp.maximum(m, jnp.max(s, axis=-1, keepdims=True))
        p = jnp.exp2(s - m_new)
        v = v_ref[pl.ds(pl.multiple_of(j * t, t), t), :]
        acc = jnp.exp2(m - m_new) * acc + _dot(p.astype(v.dtype), v)
        return m_new, acc

    def pair(i, carry):
        scores(2 * i + 1, 1)
        carry = update(2 * i, 0, carry, False)
        scores(2 * i + 2, 0)
        return update(2 * i + 1, 1, carry, False)

    def odd_tail(carry):
        scores(qi, 1)
        return update(qi, 1, update(qi - 1, 0, carry, False), True)

    def even_tail(carry):
        return update(qi, 0, carry, True)

    scores(0, 0)
    init = (jnp.full((t, 1), NEG_INF, F32), jnp.zeros((t, v_ref.shape[1]), F32))
    carry = lax.fori_loop(0, qi // 2, pair, init)
    _, acc = lax.cond(qi % 2 == 1, odd_tail, even_tail, carry)
    o_ref[...] = (acc[:, :HEAD_DIM] / acc[:, HEAD_DIM:HEAD_DIM + 1]).astype(o_ref.dtype)


def _attention(q_cat, k_cat, v, heads, t=1024):
    s = q_cat.shape[0]
    t = min(t, s)
    return pl.pallas_call(
        _attn_body,
        grid=(heads, s // t),
        in_specs=[pl.BlockSpec((t, 2 * HEAD_DIM), lambda h, i: (i, h)),
                  pl.BlockSpec((s, 2 * HEAD_DIM), lambda h, i: (0, h)),
                  pl.BlockSpec((s, 2 * HEAD_DIM), lambda h, i: (0, h))],
        out_specs=pl.BlockSpec((t, HEAD_DIM), lambda h, i: (i, h)),
        out_shape=jax.ShapeDtypeStruct((s, heads * HEAD_DIM), BF16),
        scratch_shapes=[pltpu.VMEM((2, t, t), F32)],
        compiler_params=_params("parallel", "parallel"),
        name="mla_attention",
    )(q_cat, k_cat, v)


def _router_body(h_ref, w_ref, b_ref, info_ref, cnt_ref, run_ref, *, experts):
    i = pl.program_id(0)

    @pl.when(i == 0)
    def _():
        run_ref[...] = jnp.zeros_like(run_ref)

    h = h_ref[...].astype(BF16)
    tm = h.shape[0]
    w = w_ref[...]
    w_hi = w.astype(BF16)
    rem = w - w_hi.astype(F32)
    w_mid = rem.astype(BF16)
    w_lo = (rem - w_mid.astype(F32)).astype(BF16)
    logits = _dot(h, w_hi) + _dot(h, w_mid) + _dot(h, w_lo) + b_ref[...]
    lane = lax.broadcasted_iota(jnp.int32, (tm, LANES), 1)
    logits = jnp.where(lane < experts, logits, NEG_INF)
    e = jnp.exp(logits - jnp.max(logits, axis=-1, keepdims=True))
    probs = e / jnp.sum(e, axis=-1, keepdims=True)
    probs = jnp.where(lane < experts, probs, -1.0)
    p1 = jnp.max(probs, axis=-1, keepdims=True)
    i1 = jnp.min(jnp.where(probs == p1, lane, LANES), axis=-1, keepdims=True)
    rest = jnp.where(lane == i1, -1.0, probs)
    p2 = jnp.max(rest, axis=-1, keepdims=True)
    i2 = jnp.min(jnp.where(rest == p2, lane, LANES), axis=-1, keepdims=True)
    total = p1 + p2
    hot1 = (lane == i1).astype(F32)
    hot2 = (lane == i2).astype(F32)
    hot = hot1 + hot2
    r = lax.broadcasted_iota(jnp.int32, (tm, tm), 0)
    c = lax.broadcasted_iota(jnp.int32, (tm, tm), 1)
    before = (c < r).astype(BF16)
    rank = _dot(before, hot.astype(BF16)) + run_ref[...]
    rank1 = jnp.sum(rank * hot1, axis=-1, keepdims=True)
    rank2 = jnp.sum(rank * hot2, axis=-1, keepdims=True)
    vals = (i1.astype(F32), i2.astype(F32), p1 / total, p2 / total, rank1, rank2)
    info = jnp.zeros((tm, LANES), F32)
    for idx, val in enumerate(vals):
        info = jnp.where(lane == idx, val, info)
    info_ref[...] = info
    run_ref[...] += jnp.sum(hot, axis=0, keepdims=True)
    cnt_ref[...] = run_ref[...]


def _router(h, w_pad, b_pad, experts, tm=512):
    s, d = h.shape
    tm = min(tm, s)
    return pl.pallas_call(
        functools.partial(_router_body, experts=experts),
        grid=(s // tm,),
        in_specs=[pl.BlockSpec((tm, d), lambda i: (i, 0)),
                  pl.BlockSpec((d, LANES), lambda i: (0, 0)),
                  pl.BlockSpec((1, LANES), lambda i: (0, 0))],
        out_specs=[pl.BlockSpec((tm, LANES), lambda i: (i, 0)),
                   pl.BlockSpec((1, LANES), lambda i: (0, 0))],
        out_shape=[jax.ShapeDtypeStruct((s, LANES), F32), jax.ShapeDtypeStruct((1, LANES), F32)],
        scratch_shapes=[pltpu.VMEM((1, LANES), F32)],
        compiler_params=_params("arbitrary"),
    )(h, w_pad, b_pad)


def _row_copy(src_ref, src_row, dst_ref, dst_row, sem):
    return pltpu.make_async_copy(src_ref.at[pl.ds(src_row, 1), :], dst_ref.at[pl.ds(dst_row, 1), :], sem)


def _dispatch_body(src_ref, h_ref, o_ref, buf_ref, sem):
    rows = o_ref.shape[0]
    base = pl.program_id(0) * rows

    @pl.when(pl.program_id(0) == 0)
    def _():
        buf_ref[...] = jnp.zeros_like(buf_ref)

    def start(n, carry):
        row = src_ref[base + n]

        @pl.when(row >= 0)
        def _():
            _row_copy(h_ref, row, buf_ref, n, sem).start()

        return carry

    def wait(n, carry):
        row = src_ref[base + n]

        @pl.when(row >= 0)
        def _():
            _row_copy(h_ref, row, buf_ref, n, sem).wait()

        return carry

    lax.fori_loop(0, rows, start, 0, unroll=8)
    lax.fori_loop(0, rows, wait, 0, unroll=8)
    o_ref[...] = buf_ref[...].astype(o_ref.dtype)


def _dispatch(src, h, rows=512):
    d = h.shape[1]
    padded_rows = src.shape[0]
    return pl.pallas_call(
        _dispatch_body,
        grid_spec=pltpu.PrefetchScalarGridSpec(
            num_scalar_prefetch=1,
            grid=(padded_rows // rows,),
            in_specs=[pl.BlockSpec(memory_space=pl.ANY)],
            out_specs=pl.BlockSpec((rows, d), lambda i, src: (i, 0)),
            scratch_shapes=[pltpu.VMEM((rows, d), h.dtype), pltpu.SemaphoreType.DMA(())]),
        out_shape=jax.ShapeDtypeStruct((padded_rows, d), BF16),
        compiler_params=_params("arbitrary"),
        name="moe_dispatch",
    )(src, h)


MOE_STEP_SPARE, MOE_STEP_REUSE, MOE_STEP_NEW_WEIGHTS = 0, 1, 2


def _moe_up_body(tile_ref, col_ref, exp_ref, wcol_ref, kind_ref, a_ref, wg_ref, wu_ref, o_ref, wg16_ref, wu16_ref):
    del tile_ref, col_ref, exp_ref, wcol_ref
    kind = kind_ref[pl.program_id(0)]

    @pl.when(kind == MOE_STEP_NEW_WEIGHTS)
    def _():
        wg16_ref[...] = wg_ref[...].astype(BF16)
        wu16_ref[...] = wu_ref[...].astype(BF16)

    @pl.when(kind != MOE_STEP_SPARE)
    def _():
        a = a_ref[...]
        g = _dot(a, wg16_ref[...])
        u = _dot(a, wu16_ref[...])
        o_ref[...] = (_silu(g) * u).astype(o_ref.dtype)

    @pl.when(kind == MOE_STEP_SPARE)
    def _():
        o_ref[...] = jnp.zeros_like(o_ref)


def _moe_down_body(tile_ref, col_ref, exp_ref, wcol_ref, kind_ref, a_ref, w_ref, o_ref, w16_ref):
    del tile_ref, col_ref, exp_ref, wcol_ref
    kind = kind_ref[pl.program_id(0)]

    @pl.when(kind == MOE_STEP_NEW_WEIGHTS)
    def _():
        w16_ref[...] = w_ref[...].astype(BF16)

    @pl.when(kind != MOE_STEP_SPARE)
    def _():
        o_ref[...] = _dot(a_ref[...], w16_ref[...])

    @pl.when(kind == MOE_STEP_SPARE)
    def _():
        o_ref[...] = jnp.zeros_like(o_ref)


def _moe_matmul(body, name, sched, a, weights, out_dtype, tm, tn):
    rows = a.shape[0]
    _, k, n = weights[0].shape
    wspec = pl.BlockSpec((None, k, tn), lambda s, tile, col, exp, wcol, valid: (exp[s], 0, wcol[s]))
    return pl.pallas_call(
        body,
        grid_spec=pltpu.PrefetchScalarGridSpec(
            num_scalar_prefetch=5,
            grid=(sched[0].shape[0],),
            in_specs=[pl.BlockSpec((tm, a.shape[1]), lambda s, tile, col, exp, wcol, valid: (tile[s], 0))]
            + [wspec] * len(weights),
            out_specs=pl.BlockSpec((tm, tn), lambda s, tile, col, exp, wcol, valid: (tile[s], col[s])),
            scratch_shapes=[pltpu.VMEM((k, tn), BF16)] * len(weights)),
        out_shape=jax.ShapeDtypeStruct((rows, n), out_dtype),
        compiler_params=_params("arbitrary"),
        name=name,
    )(*sched, a, *weights)


def _combine_body(slot_ref, y_ref, x_ref, info_ref, gate_ref, o_ref, buf_ref, sem, *, tokens):
    base = pl.program_id(0) * tokens

    def copy(n):
        return _row_copy(y_ref, slot_ref[base * MOE_TOP_K + n], buf_ref.at[n % MOE_TOP_K], n // MOE_TOP_K, sem)

    def start(n, carry):
        copy(n).start()
        return carry

    def wait(n, carry):
        copy(n).wait()
        return carry

    lax.fori_loop(0, tokens * MOE_TOP_K, start, 0, unroll=8)
    lax.fori_loop(0, tokens * MOE_TOP_K, wait, 0, unroll=8)
    info = info_ref[...]
    mix = info[:, 2:3] * buf_ref[0] + info[:, 3:4] * buf_ref[1]
    o_ref[...] = x_ref[...] + gate_ref[...] * mix


def _combine(slots, y, x, info, gate, tokens=256):
    s, d = x.shape
    tokens = min(tokens, s)
    return pl.pallas_call(
        functools.partial(_combine_body, tokens=tokens),
        grid_spec=pltpu.PrefetchScalarGridSpec(
            num_scalar_prefetch=1,
            grid=(s // tokens,),
            in_specs=[pl.BlockSpec(memory_space=pl.ANY),
                      pl.BlockSpec((tokens, d), lambda i, slot: (i, 0)),
                      pl.BlockSpec((tokens, LANES), lambda i, slot: (i, 0)),
                      pl.BlockSpec((1, d), lambda i, slot: (0, 0))],
            out_specs=pl.BlockSpec((tokens, d), lambda i, slot: (i, 0)),
            scratch_shapes=[pltpu.VMEM((MOE_TOP_K, tokens, d), F32), pltpu.SemaphoreType.DMA(())]),
        out_shape=jax.ShapeDtypeStruct((s, d), F32),
        compiler_params=_params("arbitrary"),
    )(slots, y, x, info, gate.reshape(1, d))


def _moe_schedule(info, counts, experts, tm, n_col):
    s = info.shape[0]
    max_tiles = (s * MOE_TOP_K) // tm + experts
    ids = info[:, 0:MOE_TOP_K].astype(jnp.int32)
    ranks = info[:, 4:4 + MOE_TOP_K].astype(jnp.int32)
    cnt = counts[0, :experts].astype(jnp.int32)
    tiles = (cnt + tm - 1) // tm
    tile_end = jnp.cumsum(tiles)
    tile_start = tile_end - tiles
    slots = ((tile_start * tm)[ids] + ranks).reshape(-1)
    token = jnp.arange(s * MOE_TOP_K, dtype=jnp.int32) // MOE_TOP_K
    src = jnp.full((max_tiles * tm,), -1, jnp.int32).at[slots].set(token)
    step_end = tile_end * n_col
    total = step_end[-1]
    steps = jnp.arange(max_tiles * n_col, dtype=jnp.int32)
    valid = steps < total
    st = jnp.minimum(steps, total - 1)
    exp = jnp.sum(st[:, None] >= step_end[None, :], axis=1)
    local = st - (step_end - tiles * n_col)[exp]
    wcol = local // tiles[exp]
    spare = steps - total
    col = jnp.where(valid, wcol, spare % n_col)
    tile = jnp.where(valid, tile_start[exp] + local % tiles[exp], tile_end[-1] + spare // n_col)
    first = (local % tiles[exp]) == 0
    kind = jnp.where(valid, jnp.where(first, MOE_STEP_NEW_WEIGHTS, MOE_STEP_REUSE), MOE_STEP_SPARE)
    sched = tuple(v.astype(jnp.int32) for v in (tile, col, exp, wcol, kind))
    return slots.astype(jnp.int32), src, sched


def _rope_tables(positions):
    inv_freq = ROPE_THETA ** (-jnp.arange(0, ROPE_DIM, 2, dtype=F32) / ROPE_DIM)
    ang = positions.astype(F32)[:, None] * inv_freq
    cos, sin = jnp.cos(ang), jnp.sin(ang)
    zero = jnp.zeros_like(cos)
    pad = jnp.zeros((cos.shape[0], LANES - ROPE_DIM), F32)
    key_tabs = (jnp.concatenate([cos, cos, pad], axis=1),
                jnp.concatenate([-sin, zero, pad], axis=1),
                jnp.concatenate([zero, sin, pad], axis=1))
    return key_tabs, jnp.concatenate([cos, cos, -sin, sin], axis=1)


def _pad_lanes(v):
    return jnp.pad(v, ((0, 0), (0, LANES - v.shape[1])))


def kernel(x, c, positions, ada_w, ada_b, norm_mix, norm_ffn, gdn_w_in, gdn_conv, gdn_a_log, gdn_dt_bias, gdn_out_norm, gdn_w_out, kv_norm, w_dkv, kv_latent_norm, w_ukv, k_nope_norm, k_rope_norm, mla_w_dq, mla_q_latent_norm, mla_w_uq, mla_q_nope_norm, mla_q_rope_norm, mla_w_out, ffn_w_gate, ffn_w_up, ffn_w_down, router_w, router_b, moe_w_gate, moe_w_up, moe_w_down):
    assert x.shape[0] == 1 and ada_w.shape[0] == 2
    _, s, d = x.shape
    heads = d // HEAD_DIM
    width = heads * HEAD_DIM
    x0 = x[0]

    mod = _ada_mod(c, ada_w, ada_b).reshape(2, ADA_CHUNKS, d)
    rope_tabs, q_rope_tab = _rope_tables(positions[0])

    sh_m, sc_m, g_m, sh_f, sc_f, g_f = (mod[0, n] for n in range(ADA_CHUNKS))
    h = _norm_mod(x0, norm_mix[0], sc_m, sh_m, name="norm_mix0")
    w_in = gdn_w_in[0].astype(BF16)
    proj = _matmul(h, w_in, name="gdn_in_proj", out_dtype=BF16, n_cols=4 * width)
    w_ba = w_in[:, 4 * width:]
    b_raw = _matmul(h, _pad_lanes(w_ba[:, :heads]), name="gdn_beta_proj", out_dtype=F32)
    a_raw = _matmul(h, _pad_lanes(w_ba[:, heads:]), name="gdn_decay_proj", out_dtype=F32)
    qkv = _gdn_conv(proj, gdn_conv[0], width)
    gcb, bb, gct = _gdn_gate(b_raw, a_raw, _pad_lanes(gdn_a_log[0][None]), _pad_lanes(gdn_dt_bias[0][None]), width)
    o = _gdn_chunk(qkv, proj, gcb, bb, gct[:heads], gdn_out_norm[0], heads)
    x1 = _matmul(o, gdn_w_out[0].astype(BF16), name="gdn_out_proj", out_dtype=F32, res=x0, gate=g_m)

    h = _norm_mod(x1, norm_ffn[0], sc_f, sh_f, name="norm_ffn0")
    hid = _swiglu_up(h, ffn_w_gate[0], ffn_w_up[0])
    ffn_dim = hid.shape[1]
    x2 = _matmul_ktiled_res(hid, ffn_w_down[0].astype(BF16), x1, g_f, tk=ffn_dim // 2)

    lat = kv_latent_norm.shape[0]
    zeros = jnp.zeros((d,), F32)
    h_kv = _norm_mod(x2, kv_norm, zeros, zeros, name="norm_kv", modulate=False)
    w_dkv16 = w_dkv.astype(BF16)
    ckv, k_rope = _kv_down(h_kv, w_dkv16[:, :lat], _pad_lanes(w_dkv16[:, lat:]), kv_latent_norm[None],
                           _pad_lanes(k_rope_norm[None]), rope_tabs)
    k_cat, v = _kv_up(ckv, w_ukv.astype(BF16), k_rope, k_nope_norm[None])

    sh_m, sc_m, g_m, sh_f, sc_f, g_f = (mod[1, n] for n in range(ADA_CHUNKS))
    h = _norm_mod(x2, norm_mix[1], sc_m, sh_m, name="norm_mix1")
    cq = _q_down(h, mla_w_dq[0].astype(BF16), mla_q_latent_norm[0])
    q_lora = cq.shape[1]
    w_uq = mla_w_uq[0].astype(BF16).reshape(q_lora, heads, HEAD_DIM + ROPE_DIM)
    half = ROPE_DIM // 2
    swap = lambda t: jnp.concatenate([t[..., half:], t[..., :half]], axis=-1)
    w_rope = w_uq[..., HEAD_DIM:]
    w_uq = jnp.concatenate([w_uq, swap(w_rope)], axis=-1).reshape(q_lora, heads * 2 * HEAD_DIM)
    g_rope = mla_q_rope_norm[0][None]
    q_cat = _q_up(cq, w_uq, mla_q_nope_norm[0][None], jnp.concatenate([g_rope, swap(g_rope)], axis=1),
                  q_rope_tab, (HEAD_DIM + ROPE_DIM) ** -0.5 * math.log2(math.e))
    o = _attention(q_cat, k_cat, v, heads)
    x3 = _matmul(o, mla_w_out[0].astype(BF16), name="mla_out_proj", out_dtype=F32, res=x2, gate=g_m)

    experts = router_w.shape[-1]
    moe_tm, moe_tn = 512, 512
    assert moe_w_gate.shape[-1] == d
    h = _norm_mod(x3, norm_ffn[1], sc_f, sh_f, name="norm_ffn1", out_dtype=F32)
    info, counts = _router(h, _pad_lanes(router_w[0]), _pad_lanes(router_b[0][None]), experts)
    slots, src, sched = _moe_schedule(info, counts, experts, moe_tm, d // moe_tn)
    sorted_h = _dispatch(src, h, moe_tm)
    hid = _moe_matmul(_moe_up_body, "moe_up", sched, sorted_h, (moe_w_gate[0], moe_w_up[0]), BF16, moe_tm, moe_tn)
    y = _moe_matmul(_moe_down_body, "moe_down", sched, hid, (moe_w_down[0],), F32, moe_tm, moe_tn)
    out = _combine(slots, y, x3, info, g_f)
    return out[None]
```
